```python
import math
import jax
import jax.numpy as jnp
from jax import lax
import numpy as np

D_MODEL = 2048
BATCH = 8
SEQ = 2048
DEPTH = 2

PLE_DIM = 256
N_EVEN = (DEPTH + 1) // 2
N_ODD = DEPTH // 2
CHUNK = 64
EPS = 1e-6

GLA_HEADS = 4
GLA_DV = D_MODEL // (2 * GLA_HEADS)
GLA_DK = GLA_DV // 2
GLA_GATE_RANK = 16
GLA_GATE_NORM = 16.0
GDN_DK = 128
GDN_DV = 128
GDN_HEADS = D_MODEL // (2 * GDN_DV)
CONV_K = 4
HGRN_DK = 128
HGRN_DV = 128
HGRN_HEADS = D_MODEL // (2 * HGRN_DV)
RWKV_HD = 64
RWKV_HEADS = D_MODEL // (2 * RWKV_HD)
RWKV_W_LORA = 96
RWKV_A_LORA = 96
RWKV_G_LORA = 256
RWKV_GN_EPS = 64e-5
PEER_HEADS = 8
PEER_DKEY = 256
PEER_NKEYS = 128
PEER_EXPERTS = PEER_NKEYS * PEER_NKEYS
PEER_TOPK = 16
PEER_BLOCK = 128

GLA_QK = GLA_HEADS * GLA_DK
GLA_VW = GLA_HEADS * GLA_DV
GDN_QK = GDN_HEADS * GDN_DK
GDN_VW = GDN_HEADS * GDN_DV
EVEN_COLS = (GLA_QK, GLA_QK, GLA_VW, GLA_VW, GLA_GATE_RANK, GDN_QK, GDN_QK, GDN_VW, GDN_VW, GDN_HEADS, GDN_HEADS)
EVEN_IN = sum(EVEN_COLS)
HG_W = HGRN_HEADS * HGRN_DK
HG_VW = HGRN_HEADS * HGRN_DV
RW_W = RWKV_HEADS * RWKV_HD
HGRN_COLS = (HG_W, HG_W, HG_VW, HG_VW)
RWKV_COLS = (RW_W, RW_W, RW_W, RWKV_W_LORA, RWKV_A_LORA, RWKV_G_LORA)
HGRN_IN = sum(HGRN_COLS)
RWKV_IN = sum(RWKV_COLS)
ODD_IN = HGRN_IN + RWKV_IN

kernel_name = 'hybrid_gla_gdn_hgrn2_rwkv7_peer_trunk'


def rmsnorm(x, g):
    x32 = x.astype(jnp.float32)
    y = x32 * lax.rsqrt(jnp.mean(x32 * x32, axis=-1, keepdims=True) + EPS)
    return (y * g.astype(jnp.float32)).astype(x.dtype)


def gated_rmsnorm(o, z, g):
    o = o * lax.rsqrt(jnp.mean(o * o, axis=-1, keepdims=True) + EPS)
    return o * g.astype(jnp.float32) * jax.nn.silu(z)


def l2norm(t):
    return t * lax.rsqrt(jnp.sum(t * t, axis=-1, keepdims=True) + EPS)


def split_cols(t, sizes):
    return jnp.split(t, np.cumsum(sizes)[:-1].tolist(), axis=-1)


def to_heads(t, n_heads):
    B, S, _ = t.shape
    return t.reshape(B, S, n_heads, -1).transpose(0, 2, 1, 3)


def from_heads(t):
    B, H, S, d = t.shape
    return t.transpose(0, 2, 1, 3).reshape(B, S, H * d)


def causal_dwconv(x, w):
    return lax.conv_general_dilated(x, w[:, None, :], window_strides=(1,), padding=[(w.shape[0] - 1, 0)],
                                    dimension_numbers=('NWC', 'WIO', 'NWC'), feature_group_count=x.shape[-1])


def token_shift(t, mu):
    prev = jnp.pad(t, ((0, 0), (1, 0), (0, 0)))[:, :-1]
    return t + (prev - t) * mu


def chunk_gla(q, k, v, log_g):
    f32 = jnp.float32
    B, H, S, dk = q.shape
    dv = v.shape[-1]
    n = S // CHUNK

    def to_chunks(t):
        return jnp.moveaxis(t.astype(f32).reshape(B, H, n, CHUNK, t.shape[-1]), 2, 0)

    causal = jnp.tril(jnp.ones((CHUNK, CHUNK), dtype=bool))[:, :, None]

    def step(state, inp):
        qc, kc, vc, gc = inp
        b = jnp.cumsum(gc, axis=2)
        diff = b[:, :, :, None, :] - b[:, :, None, :, :]
        decay = jnp.where(causal, jnp.exp(jnp.where(causal, diff, 0.0)), 0.0)
        attn = jnp.einsum('bhtd,bhsd,bhtsd->bhts', qc, kc, decay)
        o = jnp.einsum('bhtd,bhde->bhte', qc * jnp.exp(b), state) + jnp.einsum('bhts,bhse->bhte', attn, vc)
        b_last = b[:, :, -1:, :]
        state = state * jnp.exp(b_last[:, :, 0, :, None]) + jnp.einsum('bhsd,bhse->bhde', kc * jnp.exp(b_last - b), vc)
        return state, o

    _, o = lax.scan(step, jnp.zeros((B, H, dk, dv), f32), tuple(map(to_chunks, (q, k, v, log_g))))
    return jnp.moveaxis(o, 0, 2).reshape(B, H, S, dv)


def chunk_gated_delta(q, k, v, log_a, beta):
    f32 = jnp.float32
    B, H, S, dk = q.shape
    dv = v.shape[-1]
    n = S // CHUNK

    def rs(t):
        return t.astype(f32).reshape((B, H, n, CHUNK) + t.shape[3:])

    q, k, v, log_a, beta = rs(q), rs(k), rs(v), rs(log_a), rs(beta)
    b = jnp.cumsum(log_a, axis=-1)
    causal = jnp.tril(jnp.ones((CHUNK, CHUNK), dtype=bool))
    strict = jnp.tril(jnp.ones((CHUNK, CHUNK), dtype=bool), -1)
    diff = b[..., :, None] - b[..., None, :]
    L = jnp.where(causal, jnp.exp(jnp.where(causal, diff, 0.0)), 0.0)
    kb = k * beta[..., None]
    A = jnp.where(strict, jnp.einsum('bhntd,bhnsd->bhnts', kb, k) * L, 0.0)
    eye = jnp.eye(CHUNK, dtype=f32)
    T = lax.linalg.triangular_solve(eye + A, jnp.broadcast_to(eye, A.shape), left_side=True, lower=True,
                                    unit_diagonal=True)
    u = T @ (v * beta[..., None])
    w = T @ (kb * jnp.exp(b)[..., None])
    qk = jnp.einsum('bhntd,bhnsd->bhnts', q, k) * L

    def step(state, inp):
        qc, kc, uc, wc, qkc, bc = inp
        v_new = uc - wc @ state
        o = (qc * jnp.exp(bc)[..., None]) @ state + qkc @ v_new
        b_last = bc[..., -1:]
        state = state * jnp.exp(b_last)[..., None] + jnp.einsum('bhsd,bhse->bhde', kc * jnp.exp(b_last - bc)[..., None], v_new)
        return state, o

    xs = tuple(jnp.moveaxis(t, 2, 0) for t in (q, k, u, w, qk, b))
    _, o = lax.scan(step, jnp.zeros((B, H, dk, dv), f32), xs)
    return jnp.moveaxis(o, 0, 2).reshape(B, H, S, dv)


def rwkv7_scan(r, log_w, k, v, kk, a):
    B, S, H, N = r.shape

    def step(state, inp):
        r_t, w_t, k_t, v_t, kk_t, a_t = inp
        sa = jnp.einsum('bhvk,bhk->bhv', state, kk_t)
        state = (state * jnp.exp(w_t)[:, :, None, :]
                 - sa[..., None] * (kk_t * a_t)[:, :, None, :]
                 + v_t[..., None] * k_t[:, :, None, :])
        return state, jnp.einsum('bhvk,bhk->bhv', state, r_t)

    xs = tuple(jnp.moveaxis(t, 1, 0) for t in (r, log_w, k, v, kk, a))
    _, o = lax.scan(step, jnp.zeros((B, H, N, N), jnp.float32), xs)
    return jnp.moveaxis(o, 0, 1)


def even_mixer(hn, w_in, gk_w2, gk_b, gla_norm, conv_w, a_log, dt_bias, gdn_norm, w_out):
    f32 = jnp.float32
    proj = (hn @ w_in).astype(f32)
    gq, gk, gv, gz, glr, bq, bk, bv, bz, ba, bb = split_cols(proj, EVEN_COLS)
    log_g = jax.nn.log_sigmoid(glr @ gk_w2.astype(f32) + gk_b.astype(f32)) / GLA_GATE_NORM
    o_a = chunk_gla(to_heads(gq, GLA_HEADS) * GLA_DK ** -0.5, to_heads(gk, GLA_HEADS),
                    to_heads(gv, GLA_HEADS), to_heads(log_g, GLA_HEADS))
    o_a = from_heads(gated_rmsnorm(o_a, to_heads(gz, GLA_HEADS), gla_norm))
    qkv = jax.nn.silu(causal_dwconv(jnp.concatenate([bq, bk, bv], axis=-1), conv_w.astype(f32)))
    cq, ck, cv = split_cols(qkv, (GDN_QK, GDN_QK, GDN_VW))
    log_a = (-jnp.exp(a_log.astype(f32)) * jax.nn.softplus(ba + dt_bias.astype(f32))).transpose(0, 2, 1)
    beta = jax.nn.sigmoid(bb).transpose(0, 2, 1)
    o_b = chunk_gated_delta(l2norm(to_heads(cq, GDN_HEADS)) * GDN_DK ** -0.5, l2norm(to_heads(ck, GDN_HEADS)),
                            to_heads(cv, GDN_HEADS), log_a, beta)
    o_b = from_heads(gated_rmsnorm(o_b, to_heads(bz, GDN_HEADS), gdn_norm))
    y = jnp.concatenate([o_a, o_b], axis=-1)
    return y.astype(w_out.dtype) @ w_out


def odd_mixer(hn, w_in, lb, hg_norm, mu, w0, w2, a0, a2, g2, k_k, k_a, r_k, ln_w, ln_b, w_out):
    f32 = jnp.float32
    B, S, _ = hn.shape
    proj = (hn @ w_in).astype(f32)
    hq, hf, hi, hg = split_cols(proj[..., :HGRN_IN], HGRN_COLS)
    lb = lb.astype(f32)
    fgate = lb + (1.0 - lb) * jax.nn.sigmoid(hf)
    o_h = chunk_gla(to_heads(jax.nn.silu(hq), HGRN_HEADS), to_heads(1.0 - fgate, HGRN_HEADS),
                    to_heads(hi, HGRN_HEADS), to_heads(jnp.log(fgate), HGRN_HEADS))
    o_h = from_heads(gated_rmsnorm(o_h, to_heads(hg, HGRN_HEADS), hg_norm))
    def hs(t):
        return t.reshape(B, S, RWKV_HEADS, RWKV_HD)
    r, k, v, w_lr, a_lr, g_lr = split_cols(token_shift(proj[..., HGRN_IN:], mu.astype(f32)), RWKV_COLS)
    log_w = -jnp.exp(-jax.nn.softplus(-(w0.astype(f32) + jnp.tanh(w_lr) @ w2.astype(f32))) - 0.5)
    a = jax.nn.sigmoid(a0.astype(f32) + a_lr @ a2.astype(f32))
    g = jax.nn.sigmoid(g_lr) @ g2.astype(f32)
    kk = l2norm(hs(k * k_k.astype(f32)))
    k = k * (1.0 + (a - 1.0) * k_a.astype(f32))
    rh, kh, vh = hs(r), hs(k), hs(v)
    o = rwkv7_scan(rh, hs(log_w), kh, vh, kk, hs(a))
    mean = jnp.mean(o, axis=-1, keepdims=True)
    var = jnp.mean(jnp.square(o - mean), axis=-1, keepdims=True)
    o = ((o - mean) * lax.rsqrt(var + RWKV_GN_EPS)).reshape(B, S, RW_W) * ln_w.astype(f32) + ln_b.astype(f32)
    bonus = jnp.sum(rh * kh * r_k.astype(f32), axis=-1, keepdims=True) * vh
    o_r = (o + bonus.reshape(B, S, RW_W)) * g
    y = jnp.concatenate([o_h, o_r], axis=-1)
    return y.astype(w_out.dtype) @ w_out


def peer(hn, w_q, sub_keys, u, v):
    B, S, D = hn.shape
    T = B * S
    half = PEER_DKEY // 2
    x = hn.reshape(T, D)
    q = (x @ w_q).astype(jnp.float32).reshape(T, PEER_HEADS, 2, half)
    scores = jnp.einsum('thpd,hpnd->thpn', q, sub_keys.astype(jnp.float32))
    sv, si = lax.top_k(scores, PEER_TOPK)
    kk2 = PEER_TOPK * PEER_TOPK
    cand_s = (sv[:, :, 0, :, None] + sv[:, :, 1, None, :]).reshape(T, PEER_HEADS, kk2)
    cand_i = (si[:, :, 0, :, None] * PEER_NKEYS + si[:, :, 1, None, :]).reshape(T, PEER_HEADS, kk2)
    best_s, pos = lax.top_k(cand_s, PEER_TOPK)
    idx = jnp.take_along_axis(cand_i, pos, axis=-1)
    gate = jax.nn.softmax(best_s, axis=-1)
    nb = T // PEER_BLOCK

    def block(args):
        xb, ib, gb = args
        act = jax.nn.gelu(jnp.einsum('td,thkd->thk', xb, jnp.take(u, ib, axis=0)).astype(jnp.float32),
                          approximate=False)
        return jnp.einsum('thk,thkd->td', (gb * act).astype(v.dtype), jnp.take(v, ib, axis=0))

    out = lax.map(block, (x.reshape(nb, PEER_BLOCK, D),
                          idx.reshape(nb, PEER_BLOCK, PEER_HEADS, PEER_TOPK),
                          gate.reshape(nb, PEER_BLOCK, PEER_HEADS, PEER_TOPK)))
    return out.reshape(B, S, D)


def setup_inputs(seed: int = 0) -> dict:
    key = jax.random.key(seed)
    ks = iter(jax.random.split(key, 48))
    f32 = jnp.float32
    D = D_MODEL

    def nrm(shape, scale):
        return jax.random.normal(next(ks), shape, f32) * scale

    def unif(shape, lo, hi):
        return jax.random.uniform(next(ks), shape, f32, lo, hi)

    x = nrm((BATCH, SEQ, D), 1.0)
    p = nrm((DEPTH, BATCH, SEQ, PLE_DIM), 1.0)
    ev_w_in = nrm((N_EVEN, D, EVEN_IN), D ** -0.5)
    ev_gla_gk_w2 = nrm((N_EVEN, GLA_GATE_RANK, GLA_QK), GLA_GATE_RANK ** -0.5)
    ev_gla_gk_b = nrm((N_EVEN, GLA_QK), 0.1)
    ev_gla_norm = 1.0 + nrm((N_EVEN, GLA_DV), 0.02)
    ev_gdn_conv = nrm((N_EVEN, CONV_K, 2 * GDN_QK + GDN_VW), CONV_K ** -0.5)
    ev_gdn_a_log = jnp.log(unif((N_EVEN, GDN_HEADS), 1.0, 16.0))
    dt = jnp.exp(unif((N_EVEN, GDN_HEADS), math.log(1e-3), math.log(1e-1)))
    ev_gdn_dt_bias = dt + jnp.log(-jnp.expm1(-dt))
    ev_gdn_norm = 1.0 + nrm((N_EVEN, GDN_DV), 0.02)
    ev_w_out = nrm((N_EVEN, D, D), D ** -0.5)
    od_w_in = nrm((N_ODD, D, ODD_IN), D ** -0.5)
    hgrn_lb_logits = nrm((DEPTH, HG_W), 1.0)
    od_hgrn_norm = 1.0 + nrm((N_ODD, HGRN_DV), 0.02)
    od_rwkv_mu = unif((N_ODD, RWKV_IN), 0.0, 1.0)
    od_rwkv_w0 = -2.5 + nrm((N_ODD, RW_W), 0.5)
    od_rwkv_w2 = nrm((N_ODD, RWKV_W_LORA, RW_W), 0.1)
    od_rwkv_a0 = nrm((N_ODD, RW_W), 0.1)
    od_rwkv_a2 = nrm((N_ODD, RWKV_A_LORA, RW_W), RWKV_A_LORA ** -0.5)
    od_rwkv_g2 = nrm((N_ODD, RWKV_G_LORA, RW_W), RWKV_G_LORA ** -0.5)
    od_rwkv_k_k = 0.85 + nrm((N_ODD, RW_W), 0.02)
    od_rwkv_k_a = 1.0 + nrm((N_ODD, RW_W), 0.02)
    od_rwkv_r_k = nrm((N_ODD, RWKV_HEADS, RWKV_HD), 0.1)
    od_rwkv_ln_w = 1.0 + nrm((N_ODD, RW_W), 0.02)
    od_rwkv_ln_b = nrm((N_ODD, RW_W), 0.02)
    od_w_out = nrm((N_ODD, D, D), D ** -0.5)
    norm_mix = 1.0 + nrm((DEPTH, D), 0.02)
    norm_ffn = 1.0 + nrm((DEPTH, D), 0.02)
    norm_ple = 1.0 + nrm((DEPTH, D), 0.02)
    peer_w_q = nrm((DEPTH, D, PEER_HEADS * PEER_DKEY), D ** -0.5)
    peer_sub_keys = nrm((DEPTH, PEER_HEADS, 2, PEER_NKEYS, PEER_DKEY // 2), (PEER_DKEY // 2) ** -0.5)
    peer_u = nrm((DEPTH, PEER_EXPERTS, D), D ** -0.5)
    peer_v = nrm((DEPTH, PEER_EXPERTS, D), PEER_HEADS ** -0.5)
    ple_w_proj = nrm((DEPTH, PLE_DIM, D), PLE_DIM ** -0.5)
    ple_w_gate = nrm((DEPTH, D, D), D ** -0.5)
    norm_final = 1.0 + nrm((D,), 0.02)
    return {'x': x, 'p': p, 'ev_w_in': ev_w_in, 'ev_gla_gk_w2': ev_gla_gk_w2, 'ev_gla_gk_b': ev_gla_gk_b,
            'ev_gla_norm': ev_gla_norm, 'ev_gdn_conv': ev_gdn_conv, 'ev_gdn_a_log': ev_gdn_a_log,
            'ev_gdn_dt_bias': ev_gdn_dt_bias, 'ev_gdn_norm': ev_gdn_norm, 'ev_w_out': ev_w_out,
            'od_w_in': od_w_in, 'hgrn_lb_logits': hgrn_lb_logits, 'od_hgrn_norm': od_hgrn_norm,
            'od_rwkv_mu': od_rwkv_mu, 'od_rwkv_w0': od_rwkv_w0, 'od_rwkv_w2': od_rwkv_w2,
            'od_rwkv_a0': od_rwkv_a0, 'od_rwkv_a2': od_rwkv_a2, 'od_rwkv_g2': od_rwkv_g2,
            'od_rwkv_k_k': od_rwkv_k_k, 'od_rwkv_k_a': od_rwkv_k_a, 'od_rwkv_r_k': od_rwkv_r_k,
            'od_rwkv_ln_w': od_rwkv_ln_w, 'od_rwkv_ln_b': od_rwkv_ln_b, 'od_w_out': od_w_out,
            'norm_mix': norm_mix, 'norm_ffn': norm_ffn, 'norm_ple': norm_ple, 'peer_w_q': peer_w_q,
            'peer_sub_keys': peer_sub_keys, 'peer_u': peer_u, 'peer_v': peer_v,
            'ple_w_proj': ple_w_proj, 'ple_w_gate': ple_w_gate, 'norm_final': norm_final}


def reference(x, p, ev_w_in, ev_gla_gk_w2, ev_gla_gk_b, ev_gla_norm, ev_gdn_conv, ev_gdn_a_log,
              ev_gdn_dt_bias, ev_gdn_norm, ev_w_out, od_w_in, hgrn_lb_logits, od_hgrn_norm,
              od_rwkv_mu, od_rwkv_w0, od_rwkv_w2, od_rwkv_a0, od_rwkv_a2, od_rwkv_g2,
              od_rwkv_k_k, od_rwkv_k_a, od_rwkv_r_k, od_rwkv_ln_w, od_rwkv_ln_b, od_w_out,
              norm_mix, norm_ffn, norm_ple, peer_w_q, peer_sub_keys, peer_u, peer_v,
              ple_w_proj, ple_w_gate, norm_final):
    f32 = jnp.float32
    sm = jax.nn.softmax(hgrn_lb_logits.astype(f32), axis=0)
    lower_bounds = jnp.cumsum(sm, axis=0) - sm[0]
    h = x
    for i in range(DEPTH):
        j = i // 2
        hn = rmsnorm(h, norm_mix[i])
        if i % 2 == 0:
            y = even_mixer(hn, ev_w_in[j], ev_gla_gk_w2[j], ev_gla_gk_b[j], ev_gla_norm[j], ev_gdn_conv[j],
                           ev_gdn_a_log[j], ev_gdn_dt_bias[j], ev_gdn_norm[j], ev_w_out[j])
        else:
            y = odd_mixer(hn, od_w_in[j], lower_bounds[i], od_hgrn_norm[j], od_rwkv_mu[j], od_rwkv_w0[j],
                          od_rwkv_w2[j], od_rwkv_a0[j], od_rwkv_a2[j], od_rwkv_g2[j], od_rwkv_k_k[j],
                          od_rwkv_k_a[j], od_rwkv_r_k[j], od_rwkv_ln_w[j], od_rwkv_ln_b[j], od_w_out[j])
        h = h + y.astype(h.dtype)
        h = h + peer(rmsnorm(h, norm_ffn[i]), peer_w_q[i], peer_sub_keys[i], peer_u[i], peer_v[i]).astype(h.dtype)
        gate = jax.nn.sigmoid((rmsnorm(h, norm_ple[i]) @ ple_w_gate[i]).astype(f32))
        h = h + (gate * (p[i] @ ple_w_proj[i]).astype(f32)).astype(h.dtype)
    return rmsnorm(h, norm_final)
```

```python
import math
import jax
import jax.numpy as jnp
from jax import lax
import numpy as np
from jax.experimental import pallas as pl
from jax.experimental.pallas import tpu as pltpu

D_MODEL = 2048
BATCH = 8
SEQ = 2048
DEPTH = 2

PLE_DIM = 256
CHUNK = 64
EPS = 1e-6

GLA_HEADS = 4
GLA_DV = D_MODEL // (2 * GLA_HEADS)
GLA_DK = GLA_DV // 2
GLA_GATE_RANK = 16
GLA_GATE_NORM = 16.0
GDN_DK = 128
GDN_DV = 128
GDN_HEADS = D_MODEL // (2 * GDN_DV)
CONV_K = 4
HGRN_DK = 128
HGRN_DV = 128
HGRN_HEADS = D_MODEL // (2 * HGRN_DV)
RWKV_HD = 64
RWKV_HEADS = D_MODEL // (2 * RWKV_HD)
RWKV_W_LORA = 96
RWKV_A_LORA = 96
RWKV_G_LORA = 256
RWKV_GN_EPS = 64e-5
PEER_HEADS = 8
PEER_DKEY = 256
PEER_NKEYS = 128
PEER_EXPERTS = PEER_NKEYS * PEER_NKEYS
PEER_TOPK = 16
PEER_BLOCK = 128

GLA_QK = GLA_HEADS * GLA_DK
GLA_VW = GLA_HEADS * GLA_DV
GDN_QK = GDN_HEADS * GDN_DK
GDN_VW = GDN_HEADS * GDN_DV
EVEN_COLS = (GLA_QK, GLA_QK, GLA_VW, GLA_VW, GLA_GATE_RANK, GDN_QK, GDN_QK, GDN_VW, GDN_VW, GDN_HEADS, GDN_HEADS)
EVEN_IN = sum(EVEN_COLS)
HG_W = HGRN_HEADS * HGRN_DK
HG_VW = HGRN_HEADS * HGRN_DV
RW_W = RWKV_HEADS * RWKV_HD
HGRN_COLS = (HG_W, HG_W, HG_VW, HG_VW)
RWKV_COLS = (RW_W, RW_W, RW_W, RWKV_W_LORA, RWKV_A_LORA, RWKV_G_LORA)
HGRN_IN = sum(HGRN_COLS)
RWKV_IN = sum(RWKV_COLS)
ODD_IN = HGRN_IN + RWKV_IN


def rmsnorm(x, g):
    x32 = x.astype(jnp.float32)
    y = x32 * lax.rsqrt(jnp.mean(x32 * x32, axis=-1, keepdims=True) + EPS)
    return (y * g.astype(jnp.float32)).astype(x.dtype)


def gated_rmsnorm(o, z, g):
    o = o * lax.rsqrt(jnp.mean(o * o, axis=-1, keepdims=True) + EPS)
    return o * g.astype(jnp.float32) * jax.nn.silu(z)


def l2norm(t):
    return t * lax.rsqrt(jnp.sum(t * t, axis=-1, keepdims=True) + EPS)


def split_cols(t, sizes):
    return jnp.split(t, np.cumsum(sizes)[:-1].tolist(), axis=-1)


def to_heads(t, n_heads):
    B, S, _ = t.shape
    return t.reshape(B, S, n_heads, -1).transpose(0, 2, 1, 3)


def from_heads(t):
    B, H, S, d = t.shape
    return t.transpose(0, 2, 1, 3).reshape(B, S, H * d)


def causal_dwconv(x, w):
    return lax.conv_general_dilated(x, w[:, None, :], window_strides=(1,), padding=[(w.shape[0] - 1, 0)],
                                    dimension_numbers=('NWC', 'WIO', 'NWC'), feature_group_count=x.shape[-1])


def token_shift(t, mu):
    prev = jnp.pad(t, ((0, 0), (1, 0), (0, 0)))[:, :-1]
    return t + (prev - t) * mu


def chunk_gla(q, k, v, log_g):
    f32 = jnp.float32
    B, H, S, dk = q.shape
    dv = v.shape[-1]
    n = S // CHUNK

    def to_chunks(t):
        return jnp.moveaxis(t.astype(f32).reshape(B, H, n, CHUNK, t.shape[-1]), 2, 0)

    causal = jnp.tril(jnp.ones((CHUNK, CHUNK), dtype=bool))[:, :, None]

    def step(state, inp):
        qc, kc, vc, gc = inp
        b = jnp.cumsum(gc, axis=2)
        diff = b[:, :, :, None, :] - b[:, :, None, :, :]
        decay = jnp.where(causal, jnp.exp(jnp.where(causal, diff, 0.0)), 0.0)
        attn = jnp.einsum('bhtd,bhsd,bhtsd->bhts', qc, kc, decay)
        o = jnp.einsum('bhtd,bhde->bhte', qc * jnp.exp(b), state) + jnp.einsum('bhts,bhse->bhte', attn, vc)
        b_last = b[:, :, -1:, :]
        state = state * jnp.exp(b_last[:, :, 0, :, None]) + jnp.einsum('bhsd,bhse->bhde', kc * jnp.exp(b_last - b), vc)
        return state, o

    _, o = lax.scan(step, jnp.zeros((B, H, dk, dv), f32), tuple(map(to_chunks, (q, k, v, log_g))))
    return jnp.moveaxis(o, 0, 2).reshape(B, H, S, dv)


def chunk_gated_delta(q, k, v, log_a, beta):
    f32 = jnp.float32
    B, H, S, dk = q.shape
    dv = v.shape[-1]
    n = S // CHUNK

    def rs(t):
        return t.astype(f32).reshape((B, H, n, CHUNK) + t.shape[3:])

    q, k, v, log_a, beta = rs(q), rs(k), rs(v), rs(log_a), rs(beta)
    b = jnp.cumsum(log_a, axis=-1)
    causal = jnp.tril(jnp.ones((CHUNK, CHUNK), dtype=bool))
    strict = jnp.tril(jnp.ones((CHUNK, CHUNK), dtype=bool), -1)
    diff = b[..., :, None] - b[..., None, :]
    L = jnp.where(causal, jnp.exp(jnp.where(causal, diff, 0.0)), 0.0)
    kb = k * beta[..., None]
    A = jnp.where(strict, jnp.einsum('bhntd,bhnsd->bhnts', kb, k) * L, 0.0)
    eye = jnp.eye(CHUNK, dtype=f32)
    T = lax.linalg.triangular_solve(eye + A, jnp.broadcast_to(eye, A.shape), left_side=True, lower=True,
                                    unit_diagonal=True)
    u = T @ (v * beta[..., None])
    w = T @ (kb * jnp.exp(b)[..., None])
    qk = jnp.einsum('bhntd,bhnsd->bhnts', q, k) * L

    def step(state, inp):
        qc, kc, uc, wc, qkc, bc = inp
        v_new = uc - wc @ state
        o = (qc * jnp.exp(bc)[..., None]) @ state + qkc @ v_new
        b_last = bc[..., -1:]
        state = state * jnp.exp(b_last)[..., None] + jnp.einsum('bhsd,bhse->bhde', kc * jnp.exp(b_last - bc)[..., None], v_new)
        return state, o

    xs = tuple(jnp.moveaxis(t, 2, 0) for t in (q, k, u, w, qk, b))
    _, o = lax.scan(step, jnp.zeros((B, H, dk, dv), f32), xs)
    return jnp.moveaxis(o, 0, 2).reshape(B, H, S, dv)


def rwkv7_scan(r, log_w, k, v, kk, a):
    B, S, H, N = r.shape

    def step(state, inp):
        r_t, w_t, k_t, v_t, kk_t, a_t = inp
        sa = jnp.einsum('bhvk,bhk->bhv', state, kk_t)
        state = (state * jnp.exp(w_t)[:, :, None, :]
                 - sa[..., None] * (kk_t * a_t)[:, :, None, :]
                 + v_t[..., None] * k_t[:, :, None, :])
        return state, jnp.einsum('bhvk,bhk->bhv', state, r_t)

    xs = tuple(jnp.moveaxis(t, 1, 0) for t in (r, log_w, k, v, kk, a))
    _, o = lax.scan(step, jnp.zeros((B, H, N, N), jnp.float32), xs)
    return jnp.moveaxis(o, 0, 1)


def even_mixer(hn, w_in, gk_w2, gk_b, gla_norm, conv_w, a_log, dt_bias, gdn_norm, w_out):
    f32 = jnp.float32
    proj = (hn @ w_in).astype(f32)
    gq, gk, gv, gz, glr, bq, bk, bv, bz, ba, bb = split_cols(proj, EVEN_COLS)
    log_g = jax.nn.log_sigmoid(glr @ gk_w2.astype(f32) + gk_b.astype(f32)) / GLA_GATE_NORM
    o_a = chunk_gla(to_heads(gq, GLA_HEADS) * GLA_DK ** -0.5, to_heads(gk, GLA_HEADS),
                    to_heads(gv, GLA_HEADS), to_heads(log_g, GLA_HEADS))
    o_a = from_heads(gated_rmsnorm(o_a, to_heads(gz, GLA_HEADS), gla_norm))
    qkv = jax.nn.silu(causal_dwconv(jnp.concatenate([bq, bk, bv], axis=-1), conv_w.astype(f32)))
    cq, ck, cv = split_cols(qkv, (GDN_QK, GDN_QK, GDN_VW))
    log_a = (-jnp.exp(a_log.astype(f32)) * jax.nn.softplus(ba + dt_bias.astype(f32))).transpose(0, 2, 1)
    beta = jax.nn.sigmoid(bb).transpose(0, 2, 1)
    o_b = chunk_gated_delta(l2norm(to_heads(cq, GDN_HEADS)) * GDN_DK ** -0.5, l2norm(to_heads(ck, GDN_HEADS)),
                            to_heads(cv, GDN_HEADS), log_a, beta)
    o_b = from_heads(gated_rmsnorm(o_b, to_heads(bz, GDN_HEADS), gdn_norm))
    y = jnp.concatenate([o_a, o_b], axis=-1)
    return y.astype(w_out.dtype) @ w_out


def odd_mixer(hn, w_in, lb, hg_norm, mu, w0, w2, a0, a2, g2, k_k, k_a, r_k, ln_w, ln_b, w_out):
    f32 = jnp.float32
    B, S, _ = hn.shape
    proj = (hn @ w_in).astype(f32)
    hq, hf, hi, hg = split_cols(proj[..., :HGRN_IN], HGRN_COLS)
    lb = lb.astype(f32)
    fgate = lb + (1.0 - lb) * jax.nn.sigmoid(hf)
    o_h = chunk_gla(to_heads(jax.nn.silu(hq), HGRN_HEADS), to_heads(1.0 - fgate, HGRN_HEADS),
                    to_heads(hi, HGRN_HEADS), to_heads(jnp.log(fgate), HGRN_HEADS))
    o_h = from_heads(gated_rmsnorm(o_h, to_heads(hg, HGRN_HEADS), hg_norm))

    def hs(t):
        return t.reshape(B, S, RWKV_HEADS, RWKV_HD)
    r, k, v, w_lr, a_lr, g_lr = split_cols(token_shift(proj[..., HGRN_IN:], mu.astype(f32)), RWKV_COLS)
    log_w = -jnp.exp(-jax.nn.softplus(-(w0.astype(f32) + jnp.tanh(w_lr) @ w2.astype(f32))) - 0.5)
    a = jax.nn.sigmoid(a0.astype(f32) + a_lr @ a2.astype(f32))
    g = jax.nn.sigmoid(g_lr) @ g2.astype(f32)
    kk = l2norm(hs(k * k_k.astype(f32)))
    k = k * (1.0 + (a - 1.0) * k_a.astype(f32))
    rh, kh, vh = hs(r), hs(k), hs(v)
    o = rwkv7_scan(rh, hs(log_w), kh, vh, kk, hs(a))
    mean = jnp.mean(o, axis=-1, keepdims=True)
    var = jnp.mean(jnp.square(o - mean), axis=-1, keepdims=True)
    o = ((o - mean) * lax.rsqrt(var + RWKV_GN_EPS)).reshape(B, S, RW_W) * ln_w.astype(f32) + ln_b.astype(f32)
    bonus = jnp.sum(rh * kh * r_k.astype(f32), axis=-1, keepdims=True) * vh
    o_r = (o + bonus.reshape(B, S, RW_W)) * g
    y = jnp.concatenate([o_h, o_r], axis=-1)
    return y.astype(w_out.dtype) @ w_out


def peer(hn, w_q, sub_keys, u, v):
    B, S, D = hn.shape
    T = B * S
    half = PEER_DKEY // 2
    x = hn.reshape(T, D)
    q = (x @ w_q).astype(jnp.float32).reshape(T, PEER_HEADS, 2, half)
    scores = jnp.einsum('thpd,hpnd->thpn', q, sub_keys.astype(jnp.float32))
    sv, si = lax.top_k(scores, PEER_TOPK)
    kk2 = PEER_TOPK * PEER_TOPK
    cand_s = (sv[:, :, 0, :, None] + sv[:, :, 1, None, :]).reshape(T, PEER_HEADS, kk2)
    cand_i = (si[:, :, 0, :, None] * PEER_NKEYS + si[:, :, 1, None, :]).reshape(T, PEER_HEADS, kk2)
    best_s, pos = lax.top_k(cand_s, PEER_TOPK)
    idx = jnp.take_along_axis(cand_i, pos, axis=-1)
    gate = jax.nn.softmax(best_s, axis=-1)
    nb = T // PEER_BLOCK

    def block(args):
        xb, ib, gb = args
        act = jax.nn.gelu(jnp.einsum('td,thkd->thk', xb, jnp.take(u, ib, axis=0)).astype(jnp.float32),
                          approximate=False)
        return jnp.einsum('thk,thkd->td', (gb * act).astype(v.dtype), jnp.take(v, ib, axis=0))

    out = lax.map(block, (x.reshape(nb, PEER_BLOCK, D),
                          idx.reshape(nb, PEER_BLOCK, PEER_HEADS, PEER_TOPK),
                          gate.reshape(nb, PEER_BLOCK, PEER_HEADS, PEER_TOPK)))
    return out.reshape(B, S, D)


def _final_rmsnorm_body(x_ref, g_ref, o_ref):
    x = x_ref[...]
    y = x * lax.rsqrt(jnp.mean(x * x, axis=-1, keepdims=True) + EPS)
    o_ref[...] = y * g_ref[...]


def final_rmsnorm(h, g):
    B, S, D = h.shape
    x = h.reshape(B * S, D)
    rows = 512
    out = pl.pallas_call(
        _final_rmsnorm_body,
        grid=(B * S // rows,),
        in_specs=[pl.BlockSpec((rows, D), lambda i: (i, 0)), pl.BlockSpec((1, D), lambda i: (0, 0))],
        out_specs=pl.BlockSpec((rows, D), lambda i: (i, 0)),
        out_shape=jax.ShapeDtypeStruct((B * S, D), jnp.float32),
        name="final_rmsnorm",
    )(x, g.reshape(1, D))
    return out.reshape(B, S, D)


def kernel(x, p, ev_w_in, ev_gla_gk_w2, ev_gla_gk_b, ev_gla_norm, ev_gdn_conv, ev_gdn_a_log,
           ev_gdn_dt_bias, ev_gdn_norm, ev_w_out, od_w_in, hgrn_lb_logits, od_hgrn_norm,
           od_rwkv_mu, od_rwkv_w0, od_rwkv_w2, od_rwkv_a0, od_rwkv_a2, od_rwkv_g2,
           od_rwkv_k_k, od_rwkv_k_a, od_rwkv_r_k, od_rwkv_ln_w, od_rwkv_ln_b, od_w_out,
           norm_mix, norm_ffn, norm_ple, peer_w_q, peer_sub_keys, peer_u, peer_v,
           ple_w_proj, ple_w_gate, norm_final):
    f32 = jnp.float32
    sm = jax.nn.softmax(hgrn_lb_logits.astype(f32), axis=0)
    lower_bounds = jnp.cumsum(sm, axis=0) - sm[0]
    h = x
    for i in range(DEPTH):
        j = i // 2
        hn = rmsnorm(h, norm_mix[i])
        if i % 2 == 0:
            y = even_mixer(hn, ev_w_in[j], ev_gla_gk_w2[j], ev_gla_gk_b[j], ev_gla_norm[j], ev_gdn_conv[j],
                           ev_gdn_a_log[j], ev_gdn_dt_bias[j], ev_gdn_norm[j], ev_w_out[j])
        else:
            y = odd_mixer(hn, od_w_in[j], lower_bounds[i], od_hgrn_norm[j], od_rwkv_mu[j], od_rwkv_w0[j],
                          od_rwkv_w2[j], od_rwkv_a0[j], od_rwkv_a2[j], od_rwkv_g2[j], od_rwkv_k_k[j],
                          od_rwkv_k_a[j], od_rwkv_r_k[j], od_rwkv_ln_w[j], od_rwkv_ln_b[j], od_w_out[j])
        h = h + y.astype(h.dtype)
        h = h + peer(rmsnorm(h, norm_ffn[i]), peer_w_q[i], peer_sub_keys[i], peer_u[i], peer_v[i]).astype(h.dtype)
        gate = jax.nn.sigmoid((rmsnorm(h, norm_ple[i]) @ ple_w_gate[i]).astype(f32))
        h = h + (gate * (p[i] @ ple_w_proj[i]).astype(f32)).astype(h.dtype)
    return final_rmsnorm(h, norm_final)
```

```python
import math
import jax
import jax.numpy as jnp
from jax import lax
import numpy as np
from jax.experimental import pallas as pl
from jax.experimental.pallas import tpu as pltpu

D_MODEL = 2048
BATCH = 8
SEQ = 2048
DEPTH = 2

PLE_DIM = 256
CHUNK = 64
EPS = 1e-6

GLA_HEADS = 4
GLA_DV = D_MODEL // (2 * GLA_HEADS)
GLA_DK = GLA_DV // 2
GLA_GATE_RANK = 16
GLA_GATE_NORM = 16.0
GDN_DK = 128
GDN_DV = 128
GDN_HEADS = D_MODEL // (2 * GDN_DV)
CONV_K = 4
HGRN_DK = 128
HGRN_DV = 128
HGRN_HEADS = D_MODEL // (2 * HGRN_DV)
RWKV_HD = 64
RWKV_HEADS = D_MODEL // (2 * RWKV_HD)
RWKV_W_LORA = 96
RWKV_A_LORA = 96
RWKV_G_LORA = 256
RWKV_GN_EPS = 64e-5
PEER_HEADS = 8
PEER_DKEY = 256
PEER_NKEYS = 128
PEER_EXPERTS = PEER_NKEYS * PEER_NKEYS
PEER_TOPK = 16
PEER_BLOCK = 128

GLA_QK = GLA_HEADS * GLA_DK
GLA_VW = GLA_HEADS * GLA_DV
GDN_QK = GDN_HEADS * GDN_DK
GDN_VW = GDN_HEADS * GDN_DV
EVEN_COLS = (GLA_QK, GLA_QK, GLA_VW, GLA_VW, GLA_GATE_RANK, GDN_QK, GDN_QK, GDN_VW, GDN_VW, GDN_HEADS, GDN_HEADS)
EVEN_IN = sum(EVEN_COLS)
HG_W = HGRN_HEADS * HGRN_DK
HG_VW = HGRN_HEADS * HGRN_DV
RW_W = RWKV_HEADS * RWKV_HD
HGRN_COLS = (HG_W, HG_W, HG_VW, HG_VW)
RWKV_COLS = (RW_W, RW_W, RW_W, RWKV_W_LORA, RWKV_A_LORA, RWKV_G_LORA)
HGRN_IN = sum(HGRN_COLS)
RWKV_IN = sum(RWKV_COLS)
ODD_IN = HGRN_IN + RWKV_IN


def rmsnorm(x, g):
    x32 = x.astype(jnp.float32)
    y = x32 * lax.rsqrt(jnp.mean(x32 * x32, axis=-1, keepdims=True) + EPS)
    return (y * g.astype(jnp.float32)).astype(x.dtype)


def gated_rmsnorm(o, z, g):
    o = o * lax.rsqrt(jnp.mean(o * o, axis=-1, keepdims=True) + EPS)
    return o * g.astype(jnp.float32) * jax.nn.silu(z)


def l2norm(t):
    return t * lax.rsqrt(jnp.sum(t * t, axis=-1, keepdims=True) + EPS)


def split_cols(t, sizes):
    return jnp.split(t, np.cumsum(sizes)[:-1].tolist(), axis=-1)


def to_heads(t, n_heads):
    B, S, _ = t.shape
    return t.reshape(B, S, n_heads, -1).transpose(0, 2, 1, 3)


def from_heads(t):
    B, H, S, d = t.shape
    return t.transpose(0, 2, 1, 3).reshape(B, S, H * d)


def causal_dwconv(x, w):
    return lax.conv_general_dilated(x, w[:, None, :], window_strides=(1,), padding=[(w.shape[0] - 1, 0)],
                                    dimension_numbers=('NWC', 'WIO', 'NWC'), feature_group_count=x.shape[-1])


def token_shift(t, mu):
    prev = jnp.pad(t, ((0, 0), (1, 0), (0, 0)))[:, :-1]
    return t + (prev - t) * mu


def chunk_gla(q, k, v, log_g):
    f32 = jnp.float32
    B, H, S, dk = q.shape
    dv = v.shape[-1]
    n = S // CHUNK

    def to_chunks(t):
        return jnp.moveaxis(t.astype(f32).reshape(B, H, n, CHUNK, t.shape[-1]), 2, 0)

    causal = jnp.tril(jnp.ones((CHUNK, CHUNK), dtype=bool))[:, :, None]

    def step(state, inp):
        qc, kc, vc, gc = inp
        b = jnp.cumsum(gc, axis=2)
        diff = b[:, :, :, None, :] - b[:, :, None, :, :]
        decay = jnp.where(causal, jnp.exp(jnp.where(causal, diff, 0.0)), 0.0)
        attn = jnp.einsum('bhtd,bhsd,bhtsd->bhts', qc, kc, decay)
        o = jnp.einsum('bhtd,bhde->bhte', qc * jnp.exp(b), state) + jnp.einsum('bhts,bhse->bhte', attn, vc)
        b_last = b[:, :, -1:, :]
        state = state * jnp.exp(b_last[:, :, 0, :, None]) + jnp.einsum('bhsd,bhse->bhde', kc * jnp.exp(b_last - b), vc)
        return state, o

    _, o = lax.scan(step, jnp.zeros((B, H, dk, dv), f32), tuple(map(to_chunks, (q, k, v, log_g))))
    return jnp.moveaxis(o, 0, 2).reshape(B, H, S, dv)


def chunk_gated_delta(q, k, v, log_a, beta):
    f32 = jnp.float32
    B, H, S, dk = q.shape
    dv = v.shape[-1]
    n = S // CHUNK

    def rs(t):
        return t.astype(f32).reshape((B, H, n, CHUNK) + t.shape[3:])

    q, k, v, log_a, beta = rs(q), rs(k), rs(v), rs(log_a), rs(beta)
    b = jnp.cumsum(log_a, axis=-1)
    causal = jnp.tril(jnp.ones((CHUNK, CHUNK), dtype=bool))
    strict = jnp.tril(jnp.ones((CHUNK, CHUNK), dtype=bool), -1)
    diff = b[..., :, None] - b[..., None, :]
    L = jnp.where(causal, jnp.exp(jnp.where(causal, diff, 0.0)), 0.0)
    kb = k * beta[..., None]
    A = jnp.where(strict, jnp.einsum('bhntd,bhnsd->bhnts', kb, k) * L, 0.0)
    eye = jnp.eye(CHUNK, dtype=f32)
    T = lax.linalg.triangular_solve(eye + A, jnp.broadcast_to(eye, A.shape), left_side=True, lower=True,
                                    unit_diagonal=True)
    u = T @ (v * beta[..., None])
    w = T @ (kb * jnp.exp(b)[..., None])
    qk = jnp.einsum('bhntd,bhnsd->bhnts', q, k) * L

    def step(state, inp):
        qc, kc, uc, wc, qkc, bc = inp
        v_new = uc - wc @ state
        o = (qc * jnp.exp(bc)[..., None]) @ state + qkc @ v_new
        b_last = bc[..., -1:]
        state = state * jnp.exp(b_last)[..., None] + jnp.einsum('bhsd,bhse->bhde', kc * jnp.exp(b_last - bc)[..., None], v_new)
        return state, o

    xs = tuple(jnp.moveaxis(t, 2, 0) for t in (q, k, u, w, qk, b))
    _, o = lax.scan(step, jnp.zeros((B, H, dk, dv), f32), xs)
    return jnp.moveaxis(o, 0, 2).reshape(B, H, S, dv)


RWKV_TIME_BLOCK = 16


def _rwkv7_scan_body(r_ref, w_ref, k_ref, v_ref, kk_ref, a_ref, o_ref, s_ref, ew_ref, kka_ref):
    ts, n, _ = r_ref.shape

    @pl.when(pl.program_id(0) == 0)
    def _():
        s_ref[...] = jnp.zeros_like(s_ref)

    ew_ref[...] = jnp.exp(w_ref[...])
    kka_ref[...] = kk_ref[...] * a_ref[...]

    def step(t, carry):
        sa = s_ref[0] * kk_ref[t, pl.ds(0, 1), :]
        for j in range(1, n):
            sa = sa + s_ref[j] * kk_ref[t, pl.ds(j, 1), :]
        v_t = v_ref[t]
        o = None
        for j in range(n):
            s_new = (s_ref[j] * ew_ref[t, pl.ds(j, 1), :] - sa * kka_ref[t, pl.ds(j, 1), :]
                     + v_t * k_ref[t, pl.ds(j, 1), :])
            s_ref[j] = s_new
            term = s_new * r_ref[t, pl.ds(j, 1), :]
            o = term if o is None else o + term
        o_ref[t] = o
        return carry

    lax.fori_loop(0, ts, step, 0)


def rwkv7_scan(r, log_w, k, v, kk, a):
    B, S, H, N = r.shape
    L = B * H

    def to_lanes(t):
        return t.transpose(1, 3, 0, 2).reshape(S, N, L)

    ts = RWKV_TIME_BLOCK
    spec = pl.BlockSpec((ts, N, L), lambda i: (i, 0, 0))
    o = pl.pallas_call(
        _rwkv7_scan_body,
        grid=(S // ts,),
        in_specs=[spec] * 6,
        out_specs=spec,
        out_shape=jax.ShapeDtypeStruct((S, N, L), jnp.float32),
        scratch_shapes=[pltpu.VMEM((N, N, L), jnp.float32),
                        pltpu.VMEM((ts, N, L), jnp.float32),
                        pltpu.VMEM((ts, N, L), jnp.float32)],
        compiler_params=pltpu.CompilerParams(dimension_semantics=("arbitrary",)),
        name="rwkv7_scan",
    )(*(to_lanes(t) for t in (r, log_w, k, v, kk, a)))
    return o.reshape(S, N, B, H).transpose(2, 0, 3, 1)


def even_mixer(hn, w_in, gk_w2, gk_b, gla_norm, conv_w, a_log, dt_bias, gdn_norm, w_out):
    f32 = jnp.float32
    proj = (hn @ w_in).astype(f32)
    gq, gk, gv, gz, glr, bq, bk, bv, bz, ba, bb = split_cols(proj, EVEN_COLS)
    log_g = jax.nn.log_sigmoid(glr @ gk_w2.astype(f32) + gk_b.astype(f32)) / GLA_GATE_NORM
    o_a = chunk_gla(to_heads(gq, GLA_HEADS) * GLA_DK ** -0.5, to_heads(gk, GLA_HEADS),
                    to_heads(gv, GLA_HEADS), to_heads(log_g, GLA_HEADS))
    o_a = from_heads(gated_rmsnorm(o_a, to_heads(gz, GLA_HEADS), gla_norm))
    qkv = jax.nn.silu(causal_dwconv(jnp.concatenate([bq, bk, bv], axis=-1), conv_w.astype(f32)))
    cq, ck, cv = split_cols(qkv, (GDN_QK, GDN_QK, GDN_VW))
    log_a = (-jnp.exp(a_log.astype(f32)) * jax.nn.softplus(ba + dt_bias.astype(f32))).transpose(0, 2, 1)
    beta = jax.nn.sigmoid(bb).transpose(0, 2, 1)
    o_b = chunk_gated_delta(l2norm(to_heads(cq, GDN_HEADS)) * GDN_DK ** -0.5, l2norm(to_heads(ck, GDN_HEADS)),
                            to_heads(cv, GDN_HEADS), log_a, beta)
    o_b = from_heads(gated_rmsnorm(o_b, to_heads(bz, GDN_HEADS), gdn_norm))
    y = jnp.concatenate([o_a, o_b], axis=-1)
    return y.astype(w_out.dtype) @ w_out


def odd_mixer(hn, w_in, lb, hg_norm, mu, w0, w2, a0, a2, g2, k_k, k_a, r_k, ln_w, ln_b, w_out):
    f32 = jnp.float32
    B, S, _ = hn.shape
    proj = (hn @ w_in).astype(f32)
    hq, hf, hi, hg = split_cols(proj[..., :HGRN_IN], HGRN_COLS)
    lb = lb.astype(f32)
    fgate = lb + (1.0 - lb) * jax.nn.sigmoid(hf)
    o_h = chunk_gla(to_heads(jax.nn.silu(hq), HGRN_HEADS), to_heads(1.0 - fgate, HGRN_HEADS),
                    to_heads(hi, HGRN_HEADS), to_heads(jnp.log(fgate), HGRN_HEADS))
    o_h = from_heads(gated_rmsnorm(o_h, to_heads(hg, HGRN_HEADS), hg_norm))

    def hs(t):
        return t.reshape(B, S, RWKV_HEADS, RWKV_HD)
    r, k, v, w_lr, a_lr, g_lr = split_cols(token_shift(proj[..., HGRN_IN:], mu.astype(f32)), RWKV_COLS)
    log_w = -jnp.exp(-jax.nn.softplus(-(w0.astype(f32) + jnp.tanh(w_lr) @ w2.astype(f32))) - 0.5)
    a = jax.nn.sigmoid(a0.astype(f32) + a_lr @ a2.astype(f32))
    g = jax.nn.sigmoid(g_lr) @ g2.astype(f32)
    kk = l2norm(hs(k * k_k.astype(f32)))
    k = k * (1.0 + (a - 1.0) * k_a.astype(f32))
    rh, kh, vh = hs(r), hs(k), hs(v)
    o = rwkv7_scan(rh, hs(log_w), kh, vh, kk, hs(a))
    mean = jnp.mean(o, axis=-1, keepdims=True)
    var = jnp.mean(jnp.square(o - mean), axis=-1, keepdims=True)
    o = ((o - mean) * lax.rsqrt(var + RWKV_GN_EPS)).reshape(B, S, RW_W) * ln_w.astype(f32) + ln_b.astype(f32)
    bonus = jnp.sum(rh * kh * r_k.astype(f32), axis=-1, keepdims=True) * vh
    o_r = (o + bonus.reshape(B, S, RW_W)) * g
    y = jnp.concatenate([o_h, o_r], axis=-1)
    return y.astype(w_out.dtype) @ w_out


def peer(hn, w_q, sub_keys, u, v):
    B, S, D = hn.shape
    T = B * S
    half = PEER_DKEY // 2
    x = hn.reshape(T, D)
    q = (x @ w_q).astype(jnp.float32).reshape(T, PEER_HEADS, 2, half)
    scores = jnp.einsum('thpd,hpnd->thpn', q, sub_keys.astype(jnp.float32))
    sv, si = lax.top_k(scores, PEER_TOPK)
    kk2 = PEER_TOPK * PEER_TOPK
    cand_s = (sv[:, :, 0, :, None] + sv[:, :, 1, None, :]).reshape(T, PEER_HEADS, kk2)
    cand_i = (si[:, :, 0, :, None] * PEER_NKEYS + si[:, :, 1, None, :]).reshape(T, PEER_HEADS, kk2)
    best_s, pos = lax.top_k(cand_s, PEER_TOPK)
    idx = jnp.take_along_axis(cand_i, pos, axis=-1)
    gate = jax.nn.softmax(best_s, axis=-1)
    npair = PEER_HEADS * PEER_TOPK
    out = peer_experts(x, idx.reshape(T, npair).astype(jnp.int32), gate.reshape(T, npair), u, v)
    return out.reshape(B, S, D)


LANES = 128
SUBLANES = 8
PEER_TOKEN_BLOCK = 128


def _peer_experts_body(idx_hbm, x_ref, g_ref, uv_hbm, o_ref, idx_smem, buf, r_scr, sb_scr, sem, idx_sem):
    tb, rows, _ = x_ref.shape
    npair = g_ref.shape[1]
    half = rows // 2
    i = pl.program_id(0)

    idx_copy = pltpu.make_async_copy(idx_hbm.at[pl.ds(i * (tb * npair), tb * npair)], idx_smem, idx_sem)
    idx_copy.start()
    idx_copy.wait()

    def row_copy(tok, k, slot):
        e = idx_smem[tok * npair + k]
        return pltpu.make_async_copy(uv_hbm.at[e], buf.at[slot, k], sem.at[slot])

    def slot_wait(slot):
        pltpu.make_async_copy(uv_hbm.at[pl.ds(0, npair)], buf.at[slot], sem.at[slot]).wait()

    def issue_first(k, c):
        row_copy(0, k, 0).start()
        return c

    lax.fori_loop(0, npair, issue_first, 0, unroll=8)

    row_id = lax.broadcasted_iota(jnp.int32, (npair, npair), 0)
    col_id = lax.broadcasted_iota(jnp.int32, (npair, npair), 1)
    diag = row_id == col_id

    def token(t, carry):
        slot = t % 2
        nxt = jnp.minimum(t + 1, tb - 1)
        slot_wait(slot)
        x_lo = x_ref[t, pl.ds(0, half), :]
        x_hi = x_ref[t, pl.ds(half, half), :]

        def dot_pair(k, c):
            row_copy(nxt, k, 1 - slot).start()
            p = x_lo * buf[slot, k, pl.ds(0, half), :] + x_hi * buf[slot, k, pl.ds(half, half), :]
            r_scr[pl.ds(k, 1), :] = jnp.sum(p, axis=0, keepdims=True)
            return c

        lax.fori_loop(0, npair, dot_pair, 0, unroll=8)

        d = jnp.sum(r_scr[...], axis=-1, keepdims=True)
        g_col = jnp.sum(jnp.where(diag, g_ref[pl.ds(t, 1), :], 0.0), axis=-1, keepdims=True)
        s = g_col * (0.5 * d * (1.0 + lax.erf(d * (2.0 ** -0.5))))
        sb_scr[...] = jnp.broadcast_to(s, sb_scr.shape)

        def axpy_pair(k, acc):
            lo, hi = acc
            sk = sb_scr[pl.ds(k, 1), :]
            lo = lo + sk * buf[slot, k, pl.ds(rows, half), :]
            hi = hi + sk * buf[slot, k, pl.ds(rows + half, half), :]
            return lo, hi

        zero = jnp.zeros((half, LANES), jnp.float32)
        lo, hi = lax.fori_loop(0, npair, axpy_pair, (zero, zero), unroll=8)
        o_ref[t, pl.ds(0, half), :] = lo
        o_ref[t, pl.ds(half, half), :] = hi
        return carry

    lax.fori_loop(0, tb, token, 0)
    slot_wait(tb % 2)


def peer_experts(x, idx, gate, u, v):
    T, D = x.shape
    P = idx.shape[1]
    E = u.shape[0]
    rows = D // LANES
    tb = PEER_TOKEN_BLOCK
    uv = jnp.concatenate([u.reshape(E, rows, LANES), v.reshape(E, rows, LANES)], axis=1).astype(jnp.float32)
    out = pl.pallas_call(
        _peer_experts_body,
        grid=(T // tb,),
        in_specs=[pl.BlockSpec(memory_space=pl.ANY),
                  pl.BlockSpec((tb, rows, LANES), lambda i: (i, 0, 0)),
                  pl.BlockSpec((tb, P), lambda i: (i, 0)),
                  pl.BlockSpec(memory_space=pl.ANY)],
        out_specs=pl.BlockSpec((tb, rows, LANES), lambda i: (i, 0, 0)),
        out_shape=jax.ShapeDtypeStruct((T, rows, LANES), jnp.float32),
        scratch_shapes=[pltpu.SMEM((tb * P,), jnp.int32),
                        pltpu.VMEM((2, P, 2 * rows, LANES), jnp.float32),
                        pltpu.VMEM((P, LANES), jnp.float32),
                        pltpu.VMEM((P, LANES), jnp.float32),
                        pltpu.SemaphoreType.DMA((2,)),
                        pltpu.SemaphoreType.DMA(())],
        compiler_params=pltpu.CompilerParams(dimension_semantics=("arbitrary",)),
        name="peer_experts",
    )(idx.reshape(T * P), x.reshape(T, rows, LANES), gate, uv)
    return out.reshape(T, D)


def _final_rmsnorm_body(x_ref, g_ref, o_ref):
    x = x_ref[...]
    y = x * lax.rsqrt(jnp.mean(x * x, axis=-1, keepdims=True) + EPS)
    o_ref[...] = y * g_ref[...]


def final_rmsnorm(h, g):
    B, S, D = h.shape
    x = h.reshape(B * S, D)
    rows = 512
    out = pl.pallas_call(
        _final_rmsnorm_body,
        grid=(B * S // rows,),
        in_specs=[pl.BlockSpec((rows, D), lambda i: (i, 0)), pl.BlockSpec((1, D), lambda i: (0, 0))],
        out_specs=pl.BlockSpec((rows, D), lambda i: (i, 0)),
        out_shape=jax.ShapeDtypeStruct((B * S, D), jnp.float32),
        name="final_rmsnorm",
    )(x, g.reshape(1, D))
    return out.reshape(B, S, D)


def kernel(x, p, ev_w_in, ev_gla_gk_w2, ev_gla_gk_b, ev_gla_norm, ev_gdn_conv, ev_gdn_a_log,
           ev_gdn_dt_bias, ev_gdn_norm, ev_w_out, od_w_in, hgrn_lb_logits, od_hgrn_norm,
           od_rwkv_mu, od_rwkv_w0, od_rwkv_w2, od_rwkv_a0, od_rwkv_a2, od_rwkv_g2,
           od_rwkv_k_k, od_rwkv_k_a, od_rwkv_r_k, od_rwkv_ln_w, od_rwkv_ln_b, od_w_out,
           norm_mix, norm_ffn, norm_ple, peer_w_q, peer_sub_keys, peer_u, peer_v,
           ple_w_proj, ple_w_gate, norm_final):
    f32 = jnp.float32
    sm = jax.nn.softmax(hgrn_lb_logits.astype(f32), axis=0)
    lower_bounds = jnp.cumsum(sm, axis=0) - sm[0]
    h = x
    for i in range(DEPTH):
        j = i // 2
        hn = rmsnorm(h, norm_mix[i])
        if i % 2 == 0:
            y = even_mixer(hn, ev_w_in[j], ev_gla_gk_w2[j], ev_gla_gk_b[j], ev_gla_norm[j], ev_gdn_conv[j],
                           ev_gdn_a_log[j], ev_gdn_dt_bias[j], ev_gdn_norm[j], ev_w_out[j])
        else:
            y = odd_mixer(hn, od_w_in[j], lower_bounds[i], od_hgrn_norm[j], od_rwkv_mu[j], od_rwkv_w0[j],
                          od_rwkv_w2[j], od_rwkv_a0[j], od_rwkv_a2[j], od_rwkv_g2[j], od_rwkv_k_k[j],
                          od_rwkv_k_a[j], od_rwkv_r_k[j], od_rwkv_ln_w[j], od_rwkv_ln_b[j], od_w_out[j])
        h = h + y.astype(h.dtype)
        h = h + peer(rmsnorm(h, norm_ffn[i]), peer_w_q[i], peer_sub_keys[i], peer_u[i], peer_v[i]).astype(h.dtype)
        gate = jax.nn.sigmoid((rmsnorm(h, norm_ple[i]) @ ple_w_gate[i]).astype(f32))
        h = h + (gate * (p[i] @ ple_w_proj[i]).astype(f32)).astype(h.dtype)
    return final_rmsnorm(h, norm_final)
```

```python
import math
import jax
import jax.numpy as jnp
from jax import lax
import numpy as np
from jax.experimental import pallas as pl
from jax.experimental.pallas import tpu as pltpu

D_MODEL = 2048
BATCH = 8
SEQ = 2048
DEPTH = 2

PLE_DIM = 256
CHUNK = 64
EPS = 1e-6

GLA_HEADS = 4
GLA_DV = D_MODEL // (2 * GLA_HEADS)
GLA_DK = GLA_DV // 2
GLA_GATE_RANK = 16
GLA_GATE_NORM = 16.0
GDN_DK = 128
GDN_DV = 128
GDN_HEADS = D_MODEL // (2 * GDN_DV)
CONV_K = 4
HGRN_DK = 128
HGRN_DV = 128
HGRN_HEADS = D_MODEL // (2 * HGRN_DV)
RWKV_HD = 64
RWKV_HEADS = D_MODEL // (2 * RWKV_HD)
RWKV_W_LORA = 96
RWKV_A_LORA = 96
RWKV_G_LORA = 256
RWKV_GN_EPS = 64e-5
PEER_HEADS = 8
PEER_DKEY = 256
PEER_NKEYS = 128
PEER_EXPERTS = PEER_NKEYS * PEER_NKEYS
PEER_TOPK = 16
PEER_BLOCK = 128

GLA_QK = GLA_HEADS * GLA_DK
GLA_VW = GLA_HEADS * GLA_DV
GDN_QK = GDN_HEADS * GDN_DK
GDN_VW = GDN_HEADS * GDN_DV
EVEN_COLS = (GLA_QK, GLA_QK, GLA_VW, GLA_VW, GLA_GATE_RANK, GDN_QK, GDN_QK, GDN_VW, GDN_VW, GDN_HEADS, GDN_HEADS)
EVEN_IN = sum(EVEN_COLS)
HG_W = HGRN_HEADS * HGRN_DK
HG_VW = HGRN_HEADS * HGRN_DV
RW_W = RWKV_HEADS * RWKV_HD
HGRN_COLS = (HG_W, HG_W, HG_VW, HG_VW)
RWKV_COLS = (RW_W, RW_W, RW_W, RWKV_W_LORA, RWKV_A_LORA, RWKV_G_LORA)
HGRN_IN = sum(HGRN_COLS)
RWKV_IN = sum(RWKV_COLS)
ODD_IN = HGRN_IN + RWKV_IN


def rmsnorm(x, g):
    x32 = x.astype(jnp.float32)
    y = x32 * lax.rsqrt(jnp.mean(x32 * x32, axis=-1, keepdims=True) + EPS)
    return (y * g.astype(jnp.float32)).astype(x.dtype)


def gated_rmsnorm(o, z, g):
    o = o * lax.rsqrt(jnp.mean(o * o, axis=-1, keepdims=True) + EPS)
    return o * g.astype(jnp.float32) * jax.nn.silu(z)


def l2norm(t):
    return t * lax.rsqrt(jnp.sum(t * t, axis=-1, keepdims=True) + EPS)


def split_cols(t, sizes):
    return jnp.split(t, np.cumsum(sizes)[:-1].tolist(), axis=-1)


def to_heads(t, n_heads):
    B, S, _ = t.shape
    return t.reshape(B, S, n_heads, -1).transpose(0, 2, 1, 3)


def from_heads(t):
    B, H, S, d = t.shape
    return t.transpose(0, 2, 1, 3).reshape(B, S, H * d)


def causal_dwconv(x, w):
    return lax.conv_general_dilated(x, w[:, None, :], window_strides=(1,), padding=[(w.shape[0] - 1, 0)],
                                    dimension_numbers=('NWC', 'WIO', 'NWC'), feature_group_count=x.shape[-1])


def token_shift(t, mu):
    prev = jnp.pad(t, ((0, 0), (1, 0), (0, 0)))[:, :-1]
    return t + (prev - t) * mu


def chunk_gla(q, k, v, log_g):
    f32 = jnp.float32
    B, H, S, dk = q.shape
    dv = v.shape[-1]
    n = S // CHUNK

    def to_chunks(t):
        return jnp.moveaxis(t.astype(f32).reshape(B, H, n, CHUNK, t.shape[-1]), 2, 0)

    causal = jnp.tril(jnp.ones((CHUNK, CHUNK), dtype=bool))[:, :, None]

    def step(state, inp):
        qc, kc, vc, gc = inp
        b = jnp.cumsum(gc, axis=2)
        diff = b[:, :, :, None, :] - b[:, :, None, :, :]
        decay = jnp.where(causal, jnp.exp(jnp.where(causal, diff, 0.0)), 0.0)
        attn = jnp.einsum('bhtd,bhsd,bhtsd->bhts', qc, kc, decay)
        o = jnp.einsum('bhtd,bhde->bhte', qc * jnp.exp(b), state) + jnp.einsum('bhts,bhse->bhte', attn, vc)
        b_last = b[:, :, -1:, :]
        state = state * jnp.exp(b_last[:, :, 0, :, None]) + jnp.einsum('bhsd,bhse->bhde', kc * jnp.exp(b_last - b), vc)
        return state, o

    _, o = lax.scan(step, jnp.zeros((B, H, dk, dv), f32), tuple(map(to_chunks, (q, k, v, log_g))))
    return jnp.moveaxis(o, 0, 2).reshape(B, H, S, dv)


def chunk_gated_delta(q, k, v, log_a, beta):
    f32 = jnp.float32
    B, H, S, dk = q.shape
    dv = v.shape[-1]
    n = S // CHUNK

    def rs(t):
        return t.astype(f32).reshape((B, H, n, CHUNK) + t.shape[3:])

    q, k, v, log_a, beta = rs(q), rs(k), rs(v), rs(log_a), rs(beta)
    b = jnp.cumsum(log_a, axis=-1)
    causal = jnp.tril(jnp.ones((CHUNK, CHUNK), dtype=bool))
    strict = jnp.tril(jnp.ones((CHUNK, CHUNK), dtype=bool), -1)
    diff = b[..., :, None] - b[..., None, :]
    L = jnp.where(causal, jnp.exp(jnp.where(causal, diff, 0.0)), 0.0)
    kb = k * beta[..., None]
    A = jnp.where(strict, jnp.einsum('bhntd,bhnsd->bhnts', kb, k) * L, 0.0)
    eye = jnp.eye(CHUNK, dtype=f32)
    T = lax.linalg.triangular_solve(eye + A, jnp.broadcast_to(eye, A.shape), left_side=True, lower=True,
                                    unit_diagonal=True)
    u = T @ (v * beta[..., None])
    w = T @ (kb * jnp.exp(b)[..., None])
    qk = jnp.einsum('bhntd,bhnsd->bhnts', q, k) * L

    def step(state, inp):
        qc, kc, uc, wc, qkc, bc = inp
        v_new = uc - wc @ state
        o = (qc * jnp.exp(bc)[..., None]) @ state + qkc @ v_new
        b_last = bc[..., -1:]
        state = state * jnp.exp(b_last)[..., None] + jnp.einsum('bhsd,bhse->bhde', kc * jnp.exp(b_last - bc)[..., None], v_new)
        return state, o

    xs = tuple(jnp.moveaxis(t, 2, 0) for t in (q, k, u, w, qk, b))
    _, o = lax.scan(step, jnp.zeros((B, H, dk, dv), f32), xs)
    return jnp.moveaxis(o, 0, 2).reshape(B, H, S, dv)


RWKV_TIME_BLOCK = 16


def _rwkv7_scan_body(r_ref, w_ref, k_ref, v_ref, kk_ref, a_ref, o_ref, s_ref, ew_ref, kka_ref):
    ts, n, _ = r_ref.shape

    @pl.when(pl.program_id(0) == 0)
    def _():
        s_ref[...] = jnp.zeros_like(s_ref)

    ew_ref[...] = jnp.exp(w_ref[...])
    kka_ref[...] = kk_ref[...] * a_ref[...]

    def step(t, carry):
        sa = s_ref[0] * kk_ref[t, pl.ds(0, 1), :]
        for j in range(1, n):
            sa = sa + s_ref[j] * kk_ref[t, pl.ds(j, 1), :]
        v_t = v_ref[t]
        o = None
        for j in range(n):
            s_new = (s_ref[j] * ew_ref[t, pl.ds(j, 1), :] - sa * kka_ref[t, pl.ds(j, 1), :]
                     + v_t * k_ref[t, pl.ds(j, 1), :])
            s_ref[j] = s_new
            term = s_new * r_ref[t, pl.ds(j, 1), :]
            o = term if o is None else o + term
        o_ref[t] = o
        return carry

    lax.fori_loop(0, ts, step, 0)


def rwkv7_scan(r, log_w, k, v, kk, a):
    B, S, H, N = r.shape
    L = B * H

    def to_lanes(t):
        return t.transpose(1, 3, 0, 2).reshape(S, N, L)

    ts = RWKV_TIME_BLOCK
    spec = pl.BlockSpec((ts, N, L), lambda i: (i, 0, 0))
    o = pl.pallas_call(
        _rwkv7_scan_body,
        grid=(S // ts,),
        in_specs=[spec] * 6,
        out_specs=spec,
        out_shape=jax.ShapeDtypeStruct((S, N, L), jnp.float32),
        scratch_shapes=[pltpu.VMEM((N, N, L), jnp.float32),
                        pltpu.VMEM((ts, N, L), jnp.float32),
                        pltpu.VMEM((ts, N, L), jnp.float32)],
        compiler_params=pltpu.CompilerParams(dimension_semantics=("arbitrary",)),
        name="rwkv7_scan",
    )(*(to_lanes(t) for t in (r, log_w, k, v, kk, a)))
    return o.reshape(S, N, B, H).transpose(2, 0, 3, 1)


def even_mixer(hn, w_in, gk_w2, gk_b, gla_norm, conv_w, a_log, dt_bias, gdn_norm, w_out):
    f32 = jnp.float32
    proj = (hn @ w_in).astype(f32)
    gq, gk, gv, gz, glr, bq, bk, bv, bz, ba, bb = split_cols(proj, EVEN_COLS)
    log_g = jax.nn.log_sigmoid(glr @ gk_w2.astype(f32) + gk_b.astype(f32)) / GLA_GATE_NORM
    o_a = chunk_gla(to_heads(gq, GLA_HEADS) * GLA_DK ** -0.5, to_heads(gk, GLA_HEADS),
                    to_heads(gv, GLA_HEADS), to_heads(log_g, GLA_HEADS))
    o_a = from_heads(gated_rmsnorm(o_a, to_heads(gz, GLA_HEADS), gla_norm))
    qkv = jax.nn.silu(causal_dwconv(jnp.concatenate([bq, bk, bv], axis=-1), conv_w.astype(f32)))
    cq, ck, cv = split_cols(qkv, (GDN_QK, GDN_QK, GDN_VW))
    log_a = (-jnp.exp(a_log.astype(f32)) * jax.nn.softplus(ba + dt_bias.astype(f32))).transpose(0, 2, 1)
    beta = jax.nn.sigmoid(bb).transpose(0, 2, 1)
    o_b = chunk_gated_delta(l2norm(to_heads(cq, GDN_HEADS)) * GDN_DK ** -0.5, l2norm(to_heads(ck, GDN_HEADS)),
                            to_heads(cv, GDN_HEADS), log_a, beta)
    o_b = from_heads(gated_rmsnorm(o_b, to_heads(bz, GDN_HEADS), gdn_norm))
    y = jnp.concatenate([o_a, o_b], axis=-1)
    return y.astype(w_out.dtype) @ w_out


def odd_mixer(hn, w_in, lb, hg_norm, mu, w0, w2, a0, a2, g2, k_k, k_a, r_k, ln_w, ln_b, w_out):
    f32 = jnp.float32
    B, S, _ = hn.shape
    proj = (hn @ w_in).astype(f32)
    hq, hf, hi, hg = split_cols(proj[..., :HGRN_IN], HGRN_COLS)
    lb = lb.astype(f32)
    fgate = lb + (1.0 - lb) * jax.nn.sigmoid(hf)
    o_h = chunk_gla(to_heads(jax.nn.silu(hq), HGRN_HEADS), to_heads(1.0 - fgate, HGRN_HEADS),
                    to_heads(hi, HGRN_HEADS), to_heads(jnp.log(fgate), HGRN_HEADS))
    o_h = from_heads(gated_rmsnorm(o_h, to_heads(hg, HGRN_HEADS), hg_norm))

    def hs(t):
        return t.reshape(B, S, RWKV_HEADS, RWKV_HD)
    r, k, v, w_lr, a_lr, g_lr = split_cols(token_shift(proj[..., HGRN_IN:], mu.astype(f32)), RWKV_COLS)
    log_w = -jnp.exp(-jax.nn.softplus(-(w0.astype(f32) + jnp.tanh(w_lr) @ w2.astype(f32))) - 0.5)
    a = jax.nn.sigmoid(a0.astype(f32) + a_lr @ a2.astype(f32))
    g = jax.nn.sigmoid(g_lr) @ g2.astype(f32)
    kk = l2norm(hs(k * k_k.astype(f32)))
    k = k * (1.0 + (a - 1.0) * k_a.astype(f32))
    rh, kh, vh = hs(r), hs(k), hs(v)
    o = rwkv7_scan(rh, hs(log_w), kh, vh, kk, hs(a))
    mean = jnp.mean(o, axis=-1, keepdims=True)
    var = jnp.mean(jnp.square(o - mean), axis=-1, keepdims=True)
    o = ((o - mean) * lax.rsqrt(var + RWKV_GN_EPS)).reshape(B, S, RW_W) * ln_w.astype(f32) + ln_b.astype(f32)
    bonus = jnp.sum(rh * kh * r_k.astype(f32), axis=-1, keepdims=True) * vh
    o_r = (o + bonus.reshape(B, S, RW_W)) * g
    y = jnp.concatenate([o_h, o_r], axis=-1)
    return y.astype(w_out.dtype) @ w_out


LANES = 128
SUBLANES = 8
TOKENS_PER_VREG = SUBLANES * LANES


def _take_top(w_ref, n, count, emit, extra_ref=None):
    def one_round(i, prev):
        best = jnp.full((SUBLANES, LANES), -jnp.inf, jnp.float32)
        where = jnp.zeros((SUBLANES, LANES), jnp.int32)
        extra = jnp.zeros((SUBLANES, LANES), jnp.int32)
        for c in range(n):
            val = jnp.where(prev == c, -jnp.inf, w_ref[c])
            w_ref[c] = val
            better = val > best
            best = jnp.where(better, val, best)
            where = jnp.where(better, c, where)
            if extra_ref is not None:
                extra = jnp.where(better, extra_ref[c], extra)
        emit(i, best, where, extra)
        return where

    lax.fori_loop(0, count, one_round, jnp.full((SUBLANES, LANES), -1, jnp.int32))


def _peer_topk_body(s0_ref, s1_ref, idx_ref, gate_ref, w_ref, val_ref, pos_ref, cand_ref, cand_id_ref):
    nk = s0_ref.shape[0]
    topk = idx_ref.shape[0]
    for half, s_ref in enumerate((s0_ref, s1_ref)):
        w_ref[...] = s_ref[...]

        def emit_half(i, value, index, _, half=half):
            val_ref[half, i] = value
            pos_ref[half, i] = index

        _take_top(w_ref, nk, topk, emit_half)

    pairs = [(a, b) for a in range(topk) for b in range(topk) if (a + 1) * (b + 1) <= topk]
    for c, (a, b) in enumerate(pairs):
        cand_ref[c] = val_ref[0, a] + val_ref[1, b]
        cand_id_ref[c] = pos_ref[0, a] * nk + pos_ref[1, b]

    def emit_pair(i, value, _, expert):
        gate_ref[i] = value
        idx_ref[i] = expert

    _take_top(cand_ref, len(pairs), topk, emit_pair, extra_ref=cand_id_ref)

    best = gate_ref[...]
    e = jnp.exp(best - best[0:1])
    gate_ref[...] = e / jnp.sum(e, axis=0, keepdims=True)


def peer_topk(scores_t):
    H, _, NK, T = scores_t.shape
    K = PEER_TOPK
    G = T // TOKENS_PER_VREG
    npairs = sum(1 for a in range(K) for b in range(K) if (a + 1) * (b + 1) <= K)
    s = scores_t.reshape(H * 2, NK, G, SUBLANES, LANES)
    in_block = (None, NK, None, SUBLANES, LANES)
    out_block = (None, K, None, SUBLANES, LANES)
    idx_t, gate_t = pl.pallas_call(
        _peer_topk_body,
        grid=(H, G),
        in_specs=[pl.BlockSpec(in_block, lambda h, g: (2 * h, 0, g, 0, 0)),
                  pl.BlockSpec(in_block, lambda h, g: (2 * h + 1, 0, g, 0, 0))],
        out_specs=[pl.BlockSpec(out_block, lambda h, g: (h, 0, g, 0, 0)),
                   pl.BlockSpec(out_block, lambda h, g: (h, 0, g, 0, 0))],
        out_shape=[jax.ShapeDtypeStruct((H, K, G, SUBLANES, LANES), jnp.int32),
                   jax.ShapeDtypeStruct((H, K, G, SUBLANES, LANES), jnp.float32)],
        scratch_shapes=[pltpu.VMEM((NK, SUBLANES, LANES), jnp.float32),
                        pltpu.VMEM((2, K, SUBLANES, LANES), jnp.float32),
                        pltpu.VMEM((2, K, SUBLANES, LANES), jnp.int32),
                        pltpu.VMEM((npairs, SUBLANES, LANES), jnp.float32),
                        pltpu.VMEM((npairs, SUBLANES, LANES), jnp.int32)],
        compiler_params=pltpu.CompilerParams(dimension_semantics=("arbitrary", "arbitrary")),
        name="peer_topk",
    )(s, s)
    to_tokens = lambda t: t.transpose(2, 3, 4, 0, 1).reshape(T, H * K)
    return to_tokens(idx_t), to_tokens(gate_t)


def peer(hn, w_q, sub_keys, u, v):
    B, S, D = hn.shape
    T = B * S
    half = PEER_DKEY // 2
    x = hn.reshape(T, D)
    q = (x @ w_q).astype(jnp.float32).reshape(T, PEER_HEADS, 2, half)
    scores_t = jnp.einsum('thpd,hpnd->hpnt', q, sub_keys.astype(jnp.float32))
    idx, gate = peer_topk(scores_t)
    out = peer_experts(x, idx, gate, u, v)
    return out.reshape(B, S, D)
PEER_TOKEN_BLOCK = 256
PEER_SLOTS = 3
PEER_UNROLL = 8


def _peer_experts_body(idx_hbm, x_ref, g_ref, uv_hbm, o_ref, idx_smem, buf, r_scr, sb_scr, sem, idx_sem):
    tb, rows, _ = x_ref.shape
    npair = g_ref.shape[1]
    nslot = buf.shape[0]
    ahead = nslot - 1
    half = rows // 2
    nbody = npair // PEER_UNROLL
    i = pl.program_id(0)

    idx_copy = pltpu.make_async_copy(idx_hbm.at[pl.ds(i * (tb * npair), tb * npair)], idx_smem, idx_sem)
    idx_copy.start()
    idx_copy.wait()

    def row_copy(tok, k, slot):
        e = idx_smem[tok * npair + k]
        return pltpu.make_async_copy(uv_hbm.at[e], buf.at[slot, k], sem.at[slot])

    def slot_wait(slot):
        pltpu.make_async_copy(uv_hbm.at[pl.ds(0, npair)], buf.at[slot], sem.at[slot]).wait()

    for t0 in range(ahead):
        def issue_first(k, c, t0=t0):
            row_copy(t0, k, t0).start()
            return c
        lax.fori_loop(0, npair, issue_first, 0, unroll=PEER_UNROLL)

    row_id = lax.broadcasted_iota(jnp.int32, (npair, npair), 0)
    col_id = lax.broadcasted_iota(jnp.int32, (npair, npair), 1)
    diag = row_id == col_id

    def token(t, carry):
        slot = t % nslot
        nxt = jnp.minimum(t + ahead, tb - 1)
        nxt_slot = (t + ahead) % nslot
        slot_wait(slot)
        x_lo = x_ref[t, pl.ds(0, half), :]
        x_hi = x_ref[t, pl.ds(half, half), :]

        def dot_body(j, c):
            for q in range(PEER_UNROLL):
                k = j * PEER_UNROLL + q
                if q % 2 == 0:
                    row_copy(nxt, j * (PEER_UNROLL // 2) + q // 2, nxt_slot).start()
                p = x_lo * buf[slot, k, pl.ds(0, half), :] + x_hi * buf[slot, k, pl.ds(half, half), :]
                r_scr[pl.ds(k, 1), :] = jnp.sum(p, axis=0, keepdims=True)
            return c

        lax.fori_loop(0, nbody, dot_body, 0)

        d = jnp.sum(r_scr[...], axis=-1, keepdims=True)
        g_col = jnp.sum(jnp.where(diag, g_ref[pl.ds(t, 1), :], 0.0), axis=-1, keepdims=True)
        s = g_col * (0.5 * d * (1.0 + lax.erf(d * (2.0 ** -0.5))))
        sb_scr[...] = jnp.broadcast_to(s, sb_scr.shape)

        def axpy_body(j, acc):
            lo, hi = acc
            for q in range(PEER_UNROLL):
                k = j * PEER_UNROLL + q
                if q % 2 == 0:
                    row_copy(nxt, npair // 2 + j * (PEER_UNROLL // 2) + q // 2, nxt_slot).start()
                sk = sb_scr[pl.ds(k, 1), :]
                lo = lo + sk * buf[slot, k, pl.ds(rows, half), :]
                hi = hi + sk * buf[slot, k, pl.ds(rows + half, half), :]
            return lo, hi

        zero = jnp.zeros((half, LANES), jnp.float32)
        lo, hi = lax.fori_loop(0, nbody, axpy_body, (zero, zero))
        o_ref[t, pl.ds(0, half), :] = lo
        o_ref[t, pl.ds(half, half), :] = hi
        return carry

    lax.fori_loop(0, tb, token, 0)
    for extra in range(ahead):
        slot_wait((tb + extra) % nslot)


def peer_experts(x, idx, gate, u, v):
    T, D = x.shape
    P = idx.shape[1]
    E = u.shape[0]
    rows = D // LANES
    tb = PEER_TOKEN_BLOCK
    uv = jnp.concatenate([u.reshape(E, rows, LANES), v.reshape(E, rows, LANES)], axis=1).astype(jnp.float32)
    out = pl.pallas_call(
        _peer_experts_body,
        grid=(T // tb,),
        in_specs=[pl.BlockSpec(memory_space=pl.ANY),
                  pl.BlockSpec((tb, rows, LANES), lambda i: (i, 0, 0)),
                  pl.BlockSpec((tb, P), lambda i: (i, 0)),
                  pl.BlockSpec(memory_space=pl.ANY)],
        out_specs=pl.BlockSpec((tb, rows, LANES), lambda i: (i, 0, 0)),
        out_shape=jax.ShapeDtypeStruct((T, rows, LANES), jnp.float32),
        scratch_shapes=[pltpu.SMEM((tb * P,), jnp.int32),
                        pltpu.VMEM((PEER_SLOTS, P, 2 * rows, LANES), jnp.float32),
                        pltpu.VMEM((P, LANES), jnp.float32),
                        pltpu.VMEM((P, LANES), jnp.float32),
                        pltpu.SemaphoreType.DMA((PEER_SLOTS,)),
                        pltpu.SemaphoreType.DMA(())],
        compiler_params=pltpu.CompilerParams(dimension_semantics=("arbitrary",)),
        name="peer_experts",
    )(idx.reshape(T * P), x.reshape(T, rows, LANES), gate, uv)
    return out.reshape(T, D)


def _final_rmsnorm_body(x_ref, g_ref, o_ref):
    x = x_ref[...]
    y = x * lax.rsqrt(jnp.mean(x * x, axis=-1, keepdims=True) + EPS)
    o_ref[...] = y * g_ref[...]


def final_rmsnorm(h, g):
    B, S, D = h.shape
    x = h.reshape(B * S, D)
    rows = 512
    out = pl.pallas_call(
        _final_rmsnorm_body,
        grid=(B * S // rows,),
        in_specs=[pl.BlockSpec((rows, D), lambda i: (i, 0)), pl.BlockSpec((1, D), lambda i: (0, 0))],
        out_specs=pl.BlockSpec((rows, D), lambda i: (i, 0)),
        out_shape=jax.ShapeDtypeStruct((B * S, D), jnp.float32),
        name="final_rmsnorm",
    )(x, g.reshape(1, D))
    return out.reshape(B, S, D)


def kernel(x, p, ev_w_in, ev_gla_gk_w2, ev_gla_gk_b, ev_gla_norm, ev_gdn_conv, ev_gdn_a_log,
           ev_gdn_dt_bias, ev_gdn_norm, ev_w_out, od_w_in, hgrn_lb_logits, od_hgrn_norm,
           od_rwkv_mu, od_rwkv_w0, od_rwkv_w2, od_rwkv_a0, od_rwkv_a2, od_rwkv_g2,
           od_rwkv_k_k, od_rwkv_k_a, od_rwkv_r_k, od_rwkv_ln_w, od_rwkv_ln_b, od_w_out,
           norm_mix, norm_ffn, norm_ple, peer_w_q, peer_sub_keys, peer_u, peer_v,
           ple_w_proj, ple_w_gate, norm_final):
    f32 = jnp.float32
    sm = jax.nn.softmax(hgrn_lb_logits.astype(f32), axis=0)
    lower_bounds = jnp.cumsum(sm, axis=0) - sm[0]
    h = x
    for i in range(DEPTH):
        j = i // 2
        hn = rmsnorm(h, norm_mix[i])
        if i % 2 == 0:
            y = even_mixer(hn, ev_w_in[j], ev_gla_gk_w2[j], ev_gla_gk_b[j], ev_gla_norm[j], ev_gdn_conv[j],
                           ev_gdn_a_log[j], ev_gdn_dt_bias[j], ev_gdn_norm[j], ev_w_out[j])
        else:
            y = odd_mixer(hn, od_w_in[j], lower_bounds[i], od_hgrn_norm[j], od_rwkv_mu[j], od_rwkv_w0[j],
                          od_rwkv_w2[j], od_rwkv_a0[j], od_rwkv_a2[j], od_rwkv_g2[j], od_rwkv_k_k[j],
                          od_rwkv_k_a[j], od_rwkv_r_k[j], od_rwkv_ln_w[j], od_rwkv_ln_b[j], od_w_out[j])
        h = h + y.astype(h.dtype)
        h = h + peer(rmsnorm(h, norm_ffn[i]), peer_w_q[i], peer_sub_keys[i], peer_u[i], peer_v[i]).astype(h.dtype)
        gate = jax.nn.sigmoid((rmsnorm(h, norm_ple[i]) @ ple_w_gate[i]).astype(f32))
        h = h + (gate * (p[i] @ ple_w_proj[i]).astype(f32)).astype(h.dtype)
    return final_rmsnorm(h, norm_final)
```

```python
import math
import jax
import jax.numpy as jnp
from jax import lax
import numpy as np
from jax.experimental import pallas as pl
from jax.experimental.pallas import tpu as pltpu

D_MODEL = 2048
BATCH = 8
SEQ = 2048
DEPTH = 2

PLE_DIM = 256
CHUNK = 64
EPS = 1e-6

GLA_HEADS = 4
GLA_DV = D_MODEL // (2 * GLA_HEADS)
GLA_DK = GLA_DV // 2
GLA_GATE_RANK = 16
GLA_GATE_NORM = 16.0
GDN_DK = 128
GDN_DV = 128
GDN_HEADS = D_MODEL // (2 * GDN_DV)
CONV_K = 4
HGRN_DK = 128
HGRN_DV = 128
HGRN_HEADS = D_MODEL // (2 * HGRN_DV)
RWKV_HD = 64
RWKV_HEADS = D_MODEL // (2 * RWKV_HD)
RWKV_W_LORA = 96
RWKV_A_LORA = 96
RWKV_G_LORA = 256
RWKV_GN_EPS = 64e-5
PEER_HEADS = 8
PEER_DKEY = 256
PEER_NKEYS = 128
PEER_EXPERTS = PEER_NKEYS * PEER_NKEYS
PEER_TOPK = 16
PEER_BLOCK = 128

GLA_QK = GLA_HEADS * GLA_DK
GLA_VW = GLA_HEADS * GLA_DV
GDN_QK = GDN_HEADS * GDN_DK
GDN_VW = GDN_HEADS * GDN_DV
EVEN_COLS = (GLA_QK, GLA_QK, GLA_VW, GLA_VW, GLA_GATE_RANK, GDN_QK, GDN_QK, GDN_VW, GDN_VW, GDN_HEADS, GDN_HEADS)
EVEN_IN = sum(EVEN_COLS)
HG_W = HGRN_HEADS * HGRN_DK
HG_VW = HGRN_HEADS * HGRN_DV
RW_W = RWKV_HEADS * RWKV_HD
HGRN_COLS = (HG_W, HG_W, HG_VW, HG_VW)
RWKV_COLS = (RW_W, RW_W, RW_W, RWKV_W_LORA, RWKV_A_LORA, RWKV_G_LORA)
HGRN_IN = sum(HGRN_COLS)
RWKV_IN = sum(RWKV_COLS)
ODD_IN = HGRN_IN + RWKV_IN


def rmsnorm(x, g):
    x32 = x.astype(jnp.float32)
    y = x32 * lax.rsqrt(jnp.mean(x32 * x32, axis=-1, keepdims=True) + EPS)
    return (y * g.astype(jnp.float32)).astype(x.dtype)


def gated_rmsnorm(o, z, g):
    o = o * lax.rsqrt(jnp.mean(o * o, axis=-1, keepdims=True) + EPS)
    return o * g.astype(jnp.float32) * jax.nn.silu(z)


def l2norm(t):
    return t * lax.rsqrt(jnp.sum(t * t, axis=-1, keepdims=True) + EPS)


def split_cols(t, sizes):
    return jnp.split(t, np.cumsum(sizes)[:-1].tolist(), axis=-1)


def to_heads(t, n_heads):
    B, S, _ = t.shape
    return t.reshape(B, S, n_heads, -1).transpose(0, 2, 1, 3)


def from_heads(t):
    B, H, S, d = t.shape
    return t.transpose(0, 2, 1, 3).reshape(B, S, H * d)


def causal_dwconv(x, w):
    return lax.conv_general_dilated(x, w[:, None, :], window_strides=(1,), padding=[(w.shape[0] - 1, 0)],
                                    dimension_numbers=('NWC', 'WIO', 'NWC'), feature_group_count=x.shape[-1])


def token_shift(t, mu):
    prev = jnp.pad(t, ((0, 0), (1, 0), (0, 0)))[:, :-1]
    return t + (prev - t) * mu


def _chunk_gla_body(q_ref, k_ref, v_ref, g_ref, z_ref, gain_ref, o_ref, state_ref, b_ref):
    C, dk = q_ref.shape
    f32 = jnp.float32

    @pl.when(pl.program_id(2) == 0)
    def _():
        state_ref[...] = jnp.zeros_like(state_ref)

    q = q_ref[...]
    v = v_ref[...]
    row = lax.broadcasted_iota(jnp.int32, (C, C), 0)
    col = lax.broadcasted_iota(jnp.int32, (C, C), 1)
    b = jnp.dot((row >= col).astype(f32), g_ref[...], precision=lax.Precision.HIGHEST,
                preferred_element_type=f32)
    b_ref[...] = b

    t_id = lax.broadcasted_iota(jnp.int32, (C, 1), 0)
    attn = jnp.zeros((C, C), f32)
    for s in range(C):
        top = (s // SUBLANES) * SUBLANES
        keep = t_id[top:] >= s
        diff = b[top:] - b_ref[pl.ds(s, 1), :]
        m = jnp.where(keep, q[top:] * k_ref[pl.ds(s, 1), :] * jnp.exp(jnp.where(keep, diff, 0.0)), 0.0)
        colv = jnp.sum(m, axis=-1, keepdims=True)
        if top:
            colv = jnp.concatenate([jnp.zeros((top, 1), f32), colv], axis=0)
        attn = jnp.where(col == s, colv, attn)

    state_t = state_ref[...]
    o = lax.dot_general(q * jnp.exp(b), state_t, (((1,), (1,)), ((), ())), preferred_element_type=f32)
    o = o + jnp.dot(attn, v, preferred_element_type=f32)
    b_last = b_ref[pl.ds(C - 1, 1), :]
    k_dec = k_ref[...] * jnp.exp(b_last - b)
    state_ref[...] = state_t * jnp.exp(b_last) + lax.dot_general(v, k_dec, (((0,), (0,)), ((), ())),
                                                                  preferred_element_type=f32)
    z = z_ref[...]
    o = o * lax.rsqrt(jnp.mean(o * o, axis=-1, keepdims=True) + EPS)
    o_ref[...] = o * gain_ref[...] * (z * jax.nn.sigmoid(z))


def chunk_gla_gated(q, k, v, log_g, z, gain, n_heads):
    B, S, _ = q.shape
    dk = q.shape[-1] // n_heads
    dv = v.shape[-1] // n_heads
    qk_spec = pl.BlockSpec((None, CHUNK, dk), lambda b, h, c: (b, c, h))
    v_spec = pl.BlockSpec((None, CHUNK, dv), lambda b, h, c: (b, c, h))
    return pl.pallas_call(
        _chunk_gla_body,
        grid=(B, n_heads, S // CHUNK),
        in_specs=[qk_spec, qk_spec, v_spec, qk_spec, v_spec, pl.BlockSpec((1, dv), lambda b, h, c: (0, 0))],
        out_specs=v_spec,
        out_shape=jax.ShapeDtypeStruct(v.shape, jnp.float32),
        scratch_shapes=[pltpu.VMEM((dv, dk), jnp.float32), pltpu.VMEM((CHUNK, dk), jnp.float32)],
        compiler_params=pltpu.CompilerParams(dimension_semantics=("arbitrary", "arbitrary", "arbitrary")),
        name="chunk_gla",
    )(q, k, v, log_g, z, gain.reshape(1, dv).astype(jnp.float32))


def chunk_gated_delta(q, k, v, log_a, beta):
    f32 = jnp.float32
    B, H, S, dk = q.shape
    dv = v.shape[-1]
    n = S // CHUNK

    def rs(t):
        return t.astype(f32).reshape((B, H, n, CHUNK) + t.shape[3:])

    q, k, v, log_a, beta = rs(q), rs(k), rs(v), rs(log_a), rs(beta)
    b = jnp.cumsum(log_a, axis=-1)
    causal = jnp.tril(jnp.ones((CHUNK, CHUNK), dtype=bool))
    strict = jnp.tril(jnp.ones((CHUNK, CHUNK), dtype=bool), -1)
    diff = b[..., :, None] - b[..., None, :]
    L = jnp.where(causal, jnp.exp(jnp.where(causal, diff, 0.0)), 0.0)
    kb = k * beta[..., None]
    A = jnp.where(strict, jnp.einsum('bhntd,bhnsd->bhnts', kb, k) * L, 0.0)
    eye = jnp.eye(CHUNK, dtype=f32)
    T = lax.linalg.triangular_solve(eye + A, jnp.broadcast_to(eye, A.shape), left_side=True, lower=True,
                                    unit_diagonal=True)
    u = T @ (v * beta[..., None])
    w = T @ (kb * jnp.exp(b)[..., None])
    qk = jnp.einsum('bhntd,bhnsd->bhnts', q, k) * L

    def step(state, inp):
        qc, kc, uc, wc, qkc, bc = inp
        v_new = uc - wc @ state
        o = (qc * jnp.exp(bc)[..., None]) @ state + qkc @ v_new
        b_last = bc[..., -1:]
        state = state * jnp.exp(b_last)[..., None] + jnp.einsum('bhsd,bhse->bhde', kc * jnp.exp(b_last - bc)[..., None], v_new)
        return state, o

    xs = tuple(jnp.moveaxis(t, 2, 0) for t in (q, k, u, w, qk, b))
    _, o = lax.scan(step, jnp.zeros((B, H, dk, dv), f32), xs)
    return jnp.moveaxis(o, 0, 2).reshape(B, H, S, dv)


RWKV_TIME_BLOCK = 16


def _rwkv7_scan_body(r_ref, w_ref, k_ref, v_ref, kk_ref, a_ref, o_ref, s_ref, ew_ref, kka_ref):
    ts, n, _ = r_ref.shape

    @pl.when(pl.program_id(0) == 0)
    def _():
        s_ref[...] = jnp.zeros_like(s_ref)

    ew_ref[...] = jnp.exp(w_ref[...])
    kka_ref[...] = kk_ref[...] * a_ref[...]

    def step(t, carry):
        sa = s_ref[0] * kk_ref[t, pl.ds(0, 1), :]
        for j in range(1, n):
            sa = sa + s_ref[j] * kk_ref[t, pl.ds(j, 1), :]
        v_t = v_ref[t]
        o = None
        for j in range(n):
            s_new = (s_ref[j] * ew_ref[t, pl.ds(j, 1), :] - sa * kka_ref[t, pl.ds(j, 1), :]
                     + v_t * k_ref[t, pl.ds(j, 1), :])
            s_ref[j] = s_new
            term = s_new * r_ref[t, pl.ds(j, 1), :]
            o = term if o is None else o + term
        o_ref[t] = o
        return carry

    lax.fori_loop(0, ts, step, 0)


def rwkv7_scan(r, log_w, k, v, kk, a):
    B, S, H, N = r.shape
    L = B * H

    def to_lanes(t):
        return t.transpose(1, 3, 0, 2).reshape(S, N, L)

    ts = RWKV_TIME_BLOCK
    spec = pl.BlockSpec((ts, N, L), lambda i: (i, 0, 0))
    o = pl.pallas_call(
        _rwkv7_scan_body,
        grid=(S // ts,),
        in_specs=[spec] * 6,
        out_specs=spec,
        out_shape=jax.ShapeDtypeStruct((S, N, L), jnp.float32),
        scratch_shapes=[pltpu.VMEM((N, N, L), jnp.float32),
                        pltpu.VMEM((ts, N, L), jnp.float32),
                        pltpu.VMEM((ts, N, L), jnp.float32)],
        compiler_params=pltpu.CompilerParams(dimension_semantics=("arbitrary",)),
        name="rwkv7_scan",
    )(*(to_lanes(t) for t in (r, log_w, k, v, kk, a)))
    return o.reshape(S, N, B, H).transpose(2, 0, 3, 1)


def even_mixer(hn, w_in, gk_w2, gk_b, gla_norm, conv_w, a_log, dt_bias, gdn_norm, w_out):
    f32 = jnp.float32
    proj = (hn @ w_in).astype(f32)
    gq, gk, gv, gz, glr, bq, bk, bv, bz, ba, bb = split_cols(proj, EVEN_COLS)
    log_g = jax.nn.log_sigmoid(glr @ gk_w2.astype(f32) + gk_b.astype(f32)) / GLA_GATE_NORM
    o_a = chunk_gla_gated(gq * GLA_DK ** -0.5, gk, gv, log_g, gz, gla_norm, GLA_HEADS)
    qkv = jax.nn.silu(causal_dwconv(jnp.concatenate([bq, bk, bv], axis=-1), conv_w.astype(f32)))
    cq, ck, cv = split_cols(qkv, (GDN_QK, GDN_QK, GDN_VW))
    log_a = (-jnp.exp(a_log.astype(f32)) * jax.nn.softplus(ba + dt_bias.astype(f32))).transpose(0, 2, 1)
    beta = jax.nn.sigmoid(bb).transpose(0, 2, 1)
    o_b = chunk_gated_delta(l2norm(to_heads(cq, GDN_HEADS)) * GDN_DK ** -0.5, l2norm(to_heads(ck, GDN_HEADS)),
                            to_heads(cv, GDN_HEADS), log_a, beta)
    o_b = from_heads(gated_rmsnorm(o_b, to_heads(bz, GDN_HEADS), gdn_norm))
    y = jnp.concatenate([o_a, o_b], axis=-1)
    return y.astype(w_out.dtype) @ w_out


def odd_mixer(hn, w_in, lb, hg_norm, mu, w0, w2, a0, a2, g2, k_k, k_a, r_k, ln_w, ln_b, w_out):
    f32 = jnp.float32
    B, S, _ = hn.shape
    proj = (hn @ w_in).astype(f32)
    hq, hf, hi, hg = split_cols(proj[..., :HGRN_IN], HGRN_COLS)
    lb = lb.astype(f32)
    fgate = lb + (1.0 - lb) * jax.nn.sigmoid(hf)
    o_h = chunk_gla_gated(jax.nn.silu(hq), 1.0 - fgate, hi, jnp.log(fgate), hg, hg_norm, HGRN_HEADS)

    def hs(t):
        return t.reshape(B, S, RWKV_HEADS, RWKV_HD)
    r, k, v, w_lr, a_lr, g_lr = split_cols(token_shift(proj[..., HGRN_IN:], mu.astype(f32)), RWKV_COLS)
    log_w = -jnp.exp(-jax.nn.softplus(-(w0.astype(f32) + jnp.tanh(w_lr) @ w2.astype(f32))) - 0.5)
    a = jax.nn.sigmoid(a0.astype(f32) + a_lr @ a2.astype(f32))
    g = jax.nn.sigmoid(g_lr) @ g2.astype(f32)
    kk = l2norm(hs(k * k_k.astype(f32)))
    k = k * (1.0 + (a - 1.0) * k_a.astype(f32))
    rh, kh, vh = hs(r), hs(k), hs(v)
    o = rwkv7_scan(rh, hs(log_w), kh, vh, kk, hs(a))
    mean = jnp.mean(o, axis=-1, keepdims=True)
    var = jnp.mean(jnp.square(o - mean), axis=-1, keepdims=True)
    o = ((o - mean) * lax.rsqrt(var + RWKV_GN_EPS)).reshape(B, S, RW_W) * ln_w.astype(f32) + ln_b.astype(f32)
    bonus = jnp.sum(rh * kh * r_k.astype(f32), axis=-1, keepdims=True) * vh
    o_r = (o + bonus.reshape(B, S, RW_W)) * g
    y = jnp.concatenate([o_h, o_r], axis=-1)
    return y.astype(w_out.dtype) @ w_out


LANES = 128
SUBLANES = 8
TOKENS_PER_VREG = SUBLANES * LANES


def _take_top(w_ref, n, count, emit, extra_ref=None):
    def one_round(i, prev):
        best = jnp.full((SUBLANES, LANES), -jnp.inf, jnp.float32)
        where = jnp.zeros((SUBLANES, LANES), jnp.int32)
        extra = jnp.zeros((SUBLANES, LANES), jnp.int32)
        for c in range(n):
            val = jnp.where(prev == c, -jnp.inf, w_ref[c])
            w_ref[c] = val
            better = val > best
            best = jnp.where(better, val, best)
            where = jnp.where(better, c, where)
            if extra_ref is not None:
                extra = jnp.where(better, extra_ref[c], extra)
        emit(i, best, where, extra)
        return where

    lax.fori_loop(0, count, one_round, jnp.full((SUBLANES, LANES), -1, jnp.int32))


def _peer_topk_body(s0_ref, s1_ref, idx_ref, gate_ref, w_ref, val_ref, pos_ref, cand_ref, cand_id_ref):
    nk = s0_ref.shape[0]
    topk = idx_ref.shape[0]
    for half, s_ref in enumerate((s0_ref, s1_ref)):
        w_ref[...] = s_ref[...]

        def emit_half(i, value, index, _, half=half):
            val_ref[half, i] = value
            pos_ref[half, i] = index

        _take_top(w_ref, nk, topk, emit_half)

    pairs = [(a, b) for a in range(topk) for b in range(topk) if (a + 1) * (b + 1) <= topk]
    for c, (a, b) in enumerate(pairs):
        cand_ref[c] = val_ref[0, a] + val_ref[1, b]
        cand_id_ref[c] = pos_ref[0, a] * nk + pos_ref[1, b]

    def emit_pair(i, value, _, expert):
        gate_ref[i] = value
        idx_ref[i] = expert

    _take_top(cand_ref, len(pairs), topk, emit_pair, extra_ref=cand_id_ref)

    best = gate_ref[...]
    e = jnp.exp(best - best[0:1])
    gate_ref[...] = e / jnp.sum(e, axis=0, keepdims=True)


def peer_topk(scores_t):
    H, _, NK, T = scores_t.shape
    K = PEER_TOPK
    G = T // TOKENS_PER_VREG
    npairs = sum(1 for a in range(K) for b in range(K) if (a + 1) * (b + 1) <= K)
    s = scores_t.reshape(H * 2, NK, G, SUBLANES, LANES)
    in_block = (None, NK, None, SUBLANES, LANES)
    out_block = (None, K, None, SUBLANES, LANES)
    idx_t, gate_t = pl.pallas_call(
        _peer_topk_body,
        grid=(H, G),
        in_specs=[pl.BlockSpec(in_block, lambda h, g: (2 * h, 0, g, 0, 0)),
                  pl.BlockSpec(in_block, lambda h, g: (2 * h + 1, 0, g, 0, 0))],
        out_specs=[pl.BlockSpec(out_block, lambda h, g: (h, 0, g, 0, 0)),
                   pl.BlockSpec(out_block, lambda h, g: (h, 0, g, 0, 0))],
        out_shape=[jax.ShapeDtypeStruct((H, K, G, SUBLANES, LANES), jnp.int32),
                   jax.ShapeDtypeStruct((H, K, G, SUBLANES, LANES), jnp.float32)],
        scratch_shapes=[pltpu.VMEM((NK, SUBLANES, LANES), jnp.float32),
                        pltpu.VMEM((2, K, SUBLANES, LANES), jnp.float32),
                        pltpu.VMEM((2, K, SUBLANES, LANES), jnp.int32),
                        pltpu.VMEM((npairs, SUBLANES, LANES), jnp.float32),
                        pltpu.VMEM((npairs, SUBLANES, LANES), jnp.int32)],
        compiler_params=pltpu.CompilerParams(dimension_semantics=("arbitrary", "arbitrary")),
        name="peer_topk",
    )(s, s)
    to_tokens = lambda t: t.transpose(2, 3, 4, 0, 1).reshape(T, H * K)
    return to_tokens(idx_t), to_tokens(gate_t)


def peer(hn, w_q, sub_keys, u, v):
    B, S, D = hn.shape
    T = B * S
    half = PEER_DKEY // 2
    x = hn.reshape(T, D)
    q = (x @ w_q).astype(jnp.float32).reshape(T, PEER_HEADS, 2, half)
    scores_t = jnp.einsum('thpd,hpnd->hpnt', q, sub_keys.astype(jnp.float32))
    idx, gate = peer_topk(scores_t)
    out = peer_experts(x, idx, gate, u, v)
    return out.reshape(B, S, D)
PEER_TOKEN_BLOCK = 512
PEER_GROUP = SUBLANES
PEER_UNROLL = 32


def _peer_experts_body(idx_hbm, x_ref, g_ref, uv_hbm, o_ref, idx_smem, buf, r_scr, sb_scr, sem, idx_sem):
    tb, rows, _ = x_ref.shape
    npair = g_ref.shape[1]
    grp = buf.shape[0] // 2
    ngroups = tb // grp
    half = rows // 2
    nbody = npair // PEER_UNROLL
    per_body = PEER_UNROLL // 2
    i = pl.program_id(0)

    idx_copy = pltpu.make_async_copy(idx_hbm.at[pl.ds(i * (tb * npair), tb * npair)], idx_smem, idx_sem)
    idx_copy.start()
    idx_copy.wait()

    def row_copy(tok, k, slot):
        e = idx_smem[tok * npair + k]
        return pltpu.make_async_copy(uv_hbm.at[e], buf.at[slot, k], sem.at[slot])

    def slot_wait(slot):
        pltpu.make_async_copy(uv_hbm.at[pl.ds(0, npair)], buf.at[slot], sem.at[slot]).wait()

    for tok in range(grp):
        def issue_first(k, c, tok=tok):
            row_copy(tok, k, tok).start()
            return c
        lax.fori_loop(0, npair, issue_first, 0, unroll=PEER_UNROLL)

    lane = lax.broadcasted_iota(jnp.int32, (npair, LANES), 1)
    high_half = jnp.uint32(0xFFFF0000)

    def group_pair(gg, carry):
        for par in range(2):
            g = gg * 2 + par
            g_next = jnp.minimum(g + 1, ngroups - 1)
            cur, nxt = par * grp, (1 - par) * grp

            for tok in range(grp):
                t = g * grp + tok
                t_next = g_next * grp + tok
                slot_wait(cur + tok)
                x_lo = x_ref[t, pl.ds(0, half), :]
                x_hi = x_ref[t, pl.ds(half, half), :]

                def dot_body(j, c, tok=tok, t_next=t_next, x_lo=x_lo, x_hi=x_hi):
                    for q in range(PEER_UNROLL):
                        k = j * PEER_UNROLL + q
                        if q % 2 == 0:
                            row_copy(t_next, j * per_body + q // 2, nxt + tok).start()
                        u_lo = pltpu.bitcast(buf[cur + tok, k, pl.ds(0, half), :] & high_half, jnp.float32)
                        u_hi = pltpu.bitcast(buf[cur + tok, k, pl.ds(half, half), :] & high_half, jnp.float32)
                        r_scr[tok, pl.ds(k, 1), :] = jnp.sum(x_lo * u_lo + x_hi * u_hi, axis=0, keepdims=True)
                    return c

                lax.fori_loop(0, nbody, dot_body, 0)

            d = jnp.zeros((npair, LANES), jnp.float32)
            for tok in range(grp):
                d = jnp.where(lane == tok, jnp.sum(r_scr[tok], axis=-1, keepdims=True), d)
            g_tile = g_ref[pl.ds(pl.multiple_of(g * grp, grp), grp), :]
            g_cols = jnp.concatenate([g_tile, jnp.zeros((npair - grp, npair), jnp.float32)], axis=0).T
            s = g_cols * (0.5 * d * (1.0 + lax.erf(d * (2.0 ** -0.5))))
            for tok in range(grp):
                sb_scr[tok] = jnp.broadcast_to(s[:, tok:tok + 1], (npair, LANES))

            for tok in range(grp):
                t = g * grp + tok
                t_next = g_next * grp + tok

                def axpy_body(j, acc, tok=tok, t_next=t_next):
                    lo, hi = acc
                    for q in range(PEER_UNROLL):
                        k = j * PEER_UNROLL + q
                        if q % 2 == 0:
                            row_copy(t_next, npair // 2 + j * per_body + q // 2, nxt + tok).start()
                        sk = sb_scr[tok, pl.ds(k, 1), :]
                        v_lo = pltpu.bitcast(buf[cur + tok, k, pl.ds(0, half), :] << 16, jnp.float32)
                        v_hi = pltpu.bitcast(buf[cur + tok, k, pl.ds(half, half), :] << 16, jnp.float32)
                        lo = lo + sk * v_lo
                        hi = hi + sk * v_hi
                    return lo, hi

                zero = jnp.zeros((half, LANES), jnp.float32)
                lo, hi = lax.fori_loop(0, nbody, axpy_body, (zero, zero))
                o_ref[t, pl.ds(0, half), :] = lo
                o_ref[t, pl.ds(half, half), :] = hi
        return carry

    lax.fori_loop(0, ngroups // 2, group_pair, 0)
    for tok in range(grp):
        slot_wait(tok)


def pack_expert_tables(u, v):
    E, D = u.shape
    hi = lax.bitcast_convert_type(u.astype(jnp.bfloat16), jnp.uint16).astype(jnp.uint32) << 16
    lo = lax.bitcast_convert_type(v.astype(jnp.bfloat16), jnp.uint16).astype(jnp.uint32)
    return (hi | lo).reshape(E, D // LANES, LANES)


def peer_experts(x, idx, gate, u, v):
    T, D = x.shape
    P = idx.shape[1]
    rows = D // LANES
    tb = PEER_TOKEN_BLOCK
    nslot = 2 * PEER_GROUP
    out = pl.pallas_call(
        _peer_experts_body,
        grid=(T // tb,),
        in_specs=[pl.BlockSpec(memory_space=pl.ANY),
                  pl.BlockSpec((tb, rows, LANES), lambda i: (i, 0, 0)),
                  pl.BlockSpec((tb, P), lambda i: (i, 0)),
                  pl.BlockSpec(memory_space=pl.ANY)],
        out_specs=pl.BlockSpec((tb, rows, LANES), lambda i: (i, 0, 0)),
        out_shape=jax.ShapeDtypeStruct((T, rows, LANES), jnp.float32),
        scratch_shapes=[pltpu.SMEM((tb * P,), jnp.int32),
                        pltpu.VMEM((nslot, P, rows, LANES), jnp.uint32),
                        pltpu.VMEM((PEER_GROUP, P, LANES), jnp.float32),
                        pltpu.VMEM((PEER_GROUP, P, LANES), jnp.float32),
                        pltpu.SemaphoreType.DMA((nslot,)),
                        pltpu.SemaphoreType.DMA(())],
        compiler_params=pltpu.CompilerParams(dimension_semantics=("arbitrary",)),
        name="peer_experts",
    )(idx.reshape(T * P), x.reshape(T, rows, LANES), gate, pack_expert_tables(u, v))
    return out.reshape(T, D)


def _final_rmsnorm_body(x_ref, g_ref, o_ref):
    x = x_ref[...]
    y = x * lax.rsqrt(jnp.mean(x * x, axis=-1, keepdims=True) + EPS)
    o_ref[...] = y * g_ref[...]


def final_rmsnorm(h, g):
    B, S, D = h.shape
    x = h.reshape(B * S, D)
    rows = 512
    out = pl.pallas_call(
        _final_rmsnorm_body,
        grid=(B * S // rows,),
        in_specs=[pl.BlockSpec((rows, D), lambda i: (i, 0)), pl.BlockSpec((1, D), lambda i: (0, 0))],
        out_specs=pl.BlockSpec((rows, D), lambda i: (i, 0)),
        out_shape=jax.ShapeDtypeStruct((B * S, D), jnp.float32),
        name="final_rmsnorm",
    )(x, g.reshape(1, D))
    return out.reshape(B, S, D)


def kernel(x, p, ev_w_in, ev_gla_gk_w2, ev_gla_gk_b, ev_gla_norm, ev_gdn_conv, ev_gdn_a_log,
           ev_gdn_dt_bias, ev_gdn_norm, ev_w_out, od_w_in, hgrn_lb_logits, od_hgrn_norm,
           od_rwkv_mu, od_rwkv_w0, od_rwkv_w2, od_rwkv_a0, od_rwkv_a2, od_rwkv_g2,
           od_rwkv_k_k, od_rwkv_k_a, od_rwkv_r_k, od_rwkv_ln_w, od_rwkv_ln_b, od_w_out,
           norm_mix, norm_ffn, norm_ple, peer_w_q, peer_sub_keys, peer_u, peer_v,
           ple_w_proj, ple_w_gate, norm_final):
    f32 = jnp.float32
    sm = jax.nn.softmax(hgrn_lb_logits.astype(f32), axis=0)
    lower_bounds = jnp.cumsum(sm, axis=0) - sm[0]
    h = x
    for i in range(DEPTH):
        j = i // 2
        hn = rmsnorm(h, norm_mix[i])
        if i % 2 == 0:
            y = even_mixer(hn, ev_w_in[j], ev_gla_gk_w2[j], ev_gla_gk_b[j], ev_gla_norm[j], ev_gdn_conv[j],
                           ev_gdn_a_log[j], ev_gdn_dt_bias[j], ev_gdn_norm[j], ev_w_out[j])
        else:
            y = odd_mixer(hn, od_w_in[j], lower_bounds[i], od_hgrn_norm[j], od_rwkv_mu[j], od_rwkv_w0[j],
                          od_rwkv_w2[j], od_rwkv_a0[j], od_rwkv_a2[j], od_rwkv_g2[j], od_rwkv_k_k[j],
                          od_rwkv_k_a[j], od_rwkv_r_k[j], od_rwkv_ln_w[j], od_rwkv_ln_b[j], od_w_out[j])
        h = h + y.astype(h.dtype)
        h = h + peer(rmsnorm(h, norm_ffn[i]), peer_w_q[i], peer_sub_keys[i], peer_u[i], peer_v[i]).astype(h.dtype)
        gate = jax.nn.sigmoid((rmsnorm(h, norm_ple[i]) @ ple_w_gate[i]).astype(f32))
        h = h + (gate * (p[i] @ ple_w_proj[i]).astype(f32)).astype(h.dtype)
    return final_rmsnorm(h, norm_final)
```

```python
import math
import jax
import jax.numpy as jnp
from jax import lax
import numpy as np
from jax.experimental import pallas as pl
from jax.experimental.pallas import tpu as pltpu

D_MODEL = 2048
BATCH = 8
SEQ = 2048
DEPTH = 2

PLE_DIM = 256
CHUNK = 64
EPS = 1e-6

GLA_HEADS = 4
GLA_DV = D_MODEL // (2 * GLA_HEADS)
GLA_DK = GLA_DV // 2
GLA_GATE_RANK = 16
GLA_GATE_NORM = 16.0
GDN_DK = 128
GDN_DV = 128
GDN_HEADS = D_MODEL // (2 * GDN_DV)
CONV_K = 4
HGRN_DK = 128
HGRN_DV = 128
HGRN_HEADS = D_MODEL // (2 * HGRN_DV)
RWKV_HD = 64
RWKV_HEADS = D_MODEL // (2 * RWKV_HD)
RWKV_W_LORA = 96
RWKV_A_LORA = 96
RWKV_G_LORA = 256
RWKV_GN_EPS = 64e-5
PEER_HEADS = 8
PEER_DKEY = 256
PEER_NKEYS = 128
PEER_EXPERTS = PEER_NKEYS * PEER_NKEYS
PEER_TOPK = 16
PEER_BLOCK = 128

GLA_QK = GLA_HEADS * GLA_DK
GLA_VW = GLA_HEADS * GLA_DV
GDN_QK = GDN_HEADS * GDN_DK
GDN_VW = GDN_HEADS * GDN_DV
EVEN_COLS = (GLA_QK, GLA_QK, GLA_VW, GLA_VW, GLA_GATE_RANK, GDN_QK, GDN_QK, GDN_VW, GDN_VW, GDN_HEADS, GDN_HEADS)
EVEN_IN = sum(EVEN_COLS)
HG_W = HGRN_HEADS * HGRN_DK
HG_VW = HGRN_HEADS * HGRN_DV
RW_W = RWKV_HEADS * RWKV_HD
HGRN_COLS = (HG_W, HG_W, HG_VW, HG_VW)
RWKV_COLS = (RW_W, RW_W, RW_W, RWKV_W_LORA, RWKV_A_LORA, RWKV_G_LORA)
HGRN_IN = sum(HGRN_COLS)
RWKV_IN = sum(RWKV_COLS)
ODD_IN = HGRN_IN + RWKV_IN


DENSE_ROWS = 1024
DENSE_COLS = 512


def _dense_body(x_ref, w_ref, o_ref):
    o_ref[...] = jnp.dot(x_ref[...].astype(jnp.bfloat16), w_ref[...].astype(jnp.bfloat16),
                         preferred_element_type=jnp.float32)


def dense(x, w):
    lead, K = x.shape[:-1], x.shape[-1]
    N = w.shape[1]
    x2 = x.reshape(-1, K)
    M = x2.shape[0]
    tm, tn = min(DENSE_ROWS, M), min(DENSE_COLS, N)
    out = pl.pallas_call(
        _dense_body,
        grid=(pl.cdiv(M, tm), pl.cdiv(N, tn)),
        in_specs=[pl.BlockSpec((tm, K), lambda i, j: (i, 0)), pl.BlockSpec((K, tn), lambda i, j: (0, j))],
        out_specs=pl.BlockSpec((tm, tn), lambda i, j: (i, j)),
        out_shape=jax.ShapeDtypeStruct((M, N), jnp.float32),
        compiler_params=pltpu.CompilerParams(dimension_semantics=("arbitrary", "arbitrary")),
        name="dense",
    )(x2, w)
    return out.reshape(lead + (N,))


def rmsnorm(x, g):
    x32 = x.astype(jnp.float32)
    y = x32 * lax.rsqrt(jnp.mean(x32 * x32, axis=-1, keepdims=True) + EPS)
    return (y * g.astype(jnp.float32)).astype(x.dtype)


def gated_rmsnorm(o, z, g):
    o = o * lax.rsqrt(jnp.mean(o * o, axis=-1, keepdims=True) + EPS)
    return o * g.astype(jnp.float32) * jax.nn.silu(z)


def l2norm(t):
    return t * lax.rsqrt(jnp.sum(t * t, axis=-1, keepdims=True) + EPS)


def split_cols(t, sizes):
    return jnp.split(t, np.cumsum(sizes)[:-1].tolist(), axis=-1)


def to_heads(t, n_heads):
    B, S, _ = t.shape
    return t.reshape(B, S, n_heads, -1).transpose(0, 2, 1, 3)


def from_heads(t):
    B, H, S, d = t.shape
    return t.transpose(0, 2, 1, 3).reshape(B, S, H * d)


def causal_dwconv(x, w):
    return lax.conv_general_dilated(x, w[:, None, :], window_strides=(1,), padding=[(w.shape[0] - 1, 0)],
                                    dimension_numbers=('NWC', 'WIO', 'NWC'), feature_group_count=x.shape[-1])


def token_shift(t, mu):
    prev = jnp.pad(t, ((0, 0), (1, 0), (0, 0)))[:, :-1]
    return t + (prev - t) * mu


def _chunk_gla_body(q_ref, k_ref, v_ref, g_ref, z_ref, gain_ref, o_ref, state_ref, b_ref):
    C, dk = q_ref.shape
    f32 = jnp.float32

    @pl.when(pl.program_id(2) == 0)
    def _():
        state_ref[...] = jnp.zeros_like(state_ref)

    q = q_ref[...]
    v = v_ref[...]
    row = lax.broadcasted_iota(jnp.int32, (C, C), 0)
    col = lax.broadcasted_iota(jnp.int32, (C, C), 1)
    b = jnp.dot((row >= col).astype(f32), g_ref[...], precision=lax.Precision.HIGHEST,
                preferred_element_type=f32)
    b_ref[...] = b

    t_id = lax.broadcasted_iota(jnp.int32, (C, 1), 0)
    attn = jnp.zeros((C, C), f32)
    for s in range(C):
        top = (s // SUBLANES) * SUBLANES
        keep = t_id[top:] >= s
        diff = b[top:] - b_ref[pl.ds(s, 1), :]
        m = jnp.where(keep, q[top:] * k_ref[pl.ds(s, 1), :] * jnp.exp(jnp.where(keep, diff, 0.0)), 0.0)
        colv = jnp.sum(m, axis=-1, keepdims=True)
        if top:
            colv = jnp.concatenate([jnp.zeros((top, 1), f32), colv], axis=0)
        attn = jnp.where(col == s, colv, attn)

    state_t = state_ref[...]
    o = lax.dot_general(q * jnp.exp(b), state_t, (((1,), (1,)), ((), ())), preferred_element_type=f32)
    o = o + jnp.dot(attn, v, preferred_element_type=f32)
    b_last = b_ref[pl.ds(C - 1, 1), :]
    k_dec = k_ref[...] * jnp.exp(b_last - b)
    state_ref[...] = state_t * jnp.exp(b_last) + lax.dot_general(v, k_dec, (((0,), (0,)), ((), ())),
                                                                  preferred_element_type=f32)
    z = z_ref[...]
    o = o * lax.rsqrt(jnp.mean(o * o, axis=-1, keepdims=True) + EPS)
    o_ref[...] = o * gain_ref[...] * (z * jax.nn.sigmoid(z))


def chunk_gla_gated(q, k, v, log_g, z, gain, n_heads):
    B, S, _ = q.shape
    dk = q.shape[-1] // n_heads
    dv = v.shape[-1] // n_heads
    qk_spec = pl.BlockSpec((None, CHUNK, dk), lambda b, h, c: (b, c, h))
    v_spec = pl.BlockSpec((None, CHUNK, dv), lambda b, h, c: (b, c, h))
    return pl.pallas_call(
        _chunk_gla_body,
        grid=(B, n_heads, S // CHUNK),
        in_specs=[qk_spec, qk_spec, v_spec, qk_spec, v_spec, pl.BlockSpec((1, dv), lambda b, h, c: (0, 0))],
        out_specs=v_spec,
        out_shape=jax.ShapeDtypeStruct(v.shape, jnp.float32),
        scratch_shapes=[pltpu.VMEM((dv, dk), jnp.float32), pltpu.VMEM((CHUNK, dk), jnp.float32)],
        compiler_params=pltpu.CompilerParams(dimension_semantics=("arbitrary", "arbitrary", "arbitrary")),
        name="chunk_gla",
    )(q, k, v, log_g, z, gain.reshape(1, dv).astype(jnp.float32))


def chunk_gated_delta(q, k, v, log_a, beta):
    f32 = jnp.float32
    B, H, S, dk = q.shape
    dv = v.shape[-1]
    n = S // CHUNK

    def rs(t):
        return t.astype(f32).reshape((B, H, n, CHUNK) + t.shape[3:])

    q, k, v, log_a, beta = rs(q), rs(k), rs(v), rs(log_a), rs(beta)
    b = jnp.cumsum(log_a, axis=-1)
    causal = jnp.tril(jnp.ones((CHUNK, CHUNK), dtype=bool))
    strict = jnp.tril(jnp.ones((CHUNK, CHUNK), dtype=bool), -1)
    diff = b[..., :, None] - b[..., None, :]
    L = jnp.where(causal, jnp.exp(jnp.where(causal, diff, 0.0)), 0.0)
    kb = k * beta[..., None]
    A = jnp.where(strict, jnp.einsum('bhntd,bhnsd->bhnts', kb, k) * L, 0.0)
    eye = jnp.eye(CHUNK, dtype=f32)
    T = lax.linalg.triangular_solve(eye + A, jnp.broadcast_to(eye, A.shape), left_side=True, lower=True,
                                    unit_diagonal=True)
    u = T @ (v * beta[..., None])
    w = T @ (kb * jnp.exp(b)[..., None])
    qk = jnp.einsum('bhntd,bhnsd->bhnts', q, k) * L

    def step(state, inp):
        qc, kc, uc, wc, qkc, bc = inp
        v_new = uc - wc @ state
        o = (qc * jnp.exp(bc)[..., None]) @ state + qkc @ v_new
        b_last = bc[..., -1:]
        state = state * jnp.exp(b_last)[..., None] + jnp.einsum('bhsd,bhse->bhde', kc * jnp.exp(b_last - bc)[..., None], v_new)
        return state, o

    xs = tuple(jnp.moveaxis(t, 2, 0) for t in (q, k, u, w, qk, b))
    _, o = lax.scan(step, jnp.zeros((B, H, dk, dv), f32), xs)
    return jnp.moveaxis(o, 0, 2).reshape(B, H, S, dv)


RWKV_TIME_BLOCK = 16


def _rwkv7_scan_body(r_ref, w_ref, k_ref, v_ref, kk_ref, a_ref, o_ref, s_ref, ew_ref, kka_ref):
    ts, n, _ = r_ref.shape

    @pl.when(pl.program_id(0) == 0)
    def _():
        s_ref[...] = jnp.zeros_like(s_ref)

    ew_ref[...] = jnp.exp(w_ref[...])
    kka_ref[...] = kk_ref[...] * a_ref[...]

    def step(t, carry):
        sa = s_ref[0] * kk_ref[t, pl.ds(0, 1), :]
        for j in range(1, n):
            sa = sa + s_ref[j] * kk_ref[t, pl.ds(j, 1), :]
        v_t = v_ref[t]
        o = None
        for j in range(n):
            s_new = (s_ref[j] * ew_ref[t, pl.ds(j, 1), :] - sa * kka_ref[t, pl.ds(j, 1), :]
                     + v_t * k_ref[t, pl.ds(j, 1), :])
            s_ref[j] = s_new
            term = s_new * r_ref[t, pl.ds(j, 1), :]
            o = term if o is None else o + term
        o_ref[t] = o
        return carry

    lax.fori_loop(0, ts, step, 0)


def rwkv7_scan(r, log_w, k, v, kk, a):
    B, S, H, N = r.shape
    L = B * H

    def to_lanes(t):
        return t.transpose(1, 3, 0, 2).reshape(S, N, L)

    ts = RWKV_TIME_BLOCK
    spec = pl.BlockSpec((ts, N, L), lambda i: (i, 0, 0))
    o = pl.pallas_call(
        _rwkv7_scan_body,
        grid=(S // ts,),
        in_specs=[spec] * 6,
        out_specs=spec,
        out_shape=jax.ShapeDtypeStruct((S, N, L), jnp.float32),
        scratch_shapes=[pltpu.VMEM((N, N, L), jnp.float32),
                        pltpu.VMEM((ts, N, L), jnp.float32),
                        pltpu.VMEM((ts, N, L), jnp.float32)],
        compiler_params=pltpu.CompilerParams(dimension_semantics=("arbitrary",)),
        name="rwkv7_scan",
    )(*(to_lanes(t) for t in (r, log_w, k, v, kk, a)))
    return o.reshape(S, N, B, H).transpose(2, 0, 3, 1)


def even_mixer(hn, w_in, gk_w2, gk_b, gla_norm, conv_w, a_log, dt_bias, gdn_norm, w_out):
    f32 = jnp.float32
    proj = dense(hn, w_in)
    gq, gk, gv, gz, glr, bq, bk, bv, bz, ba, bb = split_cols(proj, EVEN_COLS)
    log_g = jax.nn.log_sigmoid(dense(glr, gk_w2) + gk_b.astype(f32)) / GLA_GATE_NORM
    o_a = chunk_gla_gated(gq * GLA_DK ** -0.5, gk, gv, log_g, gz, gla_norm, GLA_HEADS)
    qkv = jax.nn.silu(causal_dwconv(jnp.concatenate([bq, bk, bv], axis=-1), conv_w.astype(f32)))
    cq, ck, cv = split_cols(qkv, (GDN_QK, GDN_QK, GDN_VW))
    log_a = (-jnp.exp(a_log.astype(f32)) * jax.nn.softplus(ba + dt_bias.astype(f32))).transpose(0, 2, 1)
    beta = jax.nn.sigmoid(bb).transpose(0, 2, 1)
    o_b = chunk_gated_delta(l2norm(to_heads(cq, GDN_HEADS)) * GDN_DK ** -0.5, l2norm(to_heads(ck, GDN_HEADS)),
                            to_heads(cv, GDN_HEADS), log_a, beta)
    o_b = from_heads(gated_rmsnorm(o_b, to_heads(bz, GDN_HEADS), gdn_norm))
    y = jnp.concatenate([o_a, o_b], axis=-1)
    return dense(y, w_out)


def odd_mixer(hn, w_in, lb, hg_norm, mu, w0, w2, a0, a2, g2, k_k, k_a, r_k, ln_w, ln_b, w_out):
    f32 = jnp.float32
    B, S, _ = hn.shape
    proj = dense(hn, w_in)
    hq, hf, hi, hg = split_cols(proj[..., :HGRN_IN], HGRN_COLS)
    lb = lb.astype(f32)
    fgate = lb + (1.0 - lb) * jax.nn.sigmoid(hf)
    o_h = chunk_gla_gated(jax.nn.silu(hq), 1.0 - fgate, hi, jnp.log(fgate), hg, hg_norm, HGRN_HEADS)

    def hs(t):
        return t.reshape(B, S, RWKV_HEADS, RWKV_HD)
    r, k, v, w_lr, a_lr, g_lr = split_cols(token_shift(proj[..., HGRN_IN:], mu.astype(f32)), RWKV_COLS)
    log_w = -jnp.exp(-jax.nn.softplus(-(w0.astype(f32) + dense(jnp.tanh(w_lr), w2))) - 0.5)
    a = jax.nn.sigmoid(a0.astype(f32) + dense(a_lr, a2))
    g = dense(jax.nn.sigmoid(g_lr), g2)
    kk = l2norm(hs(k * k_k.astype(f32)))
    k = k * (1.0 + (a - 1.0) * k_a.astype(f32))
    rh, kh, vh = hs(r), hs(k), hs(v)
    o = rwkv7_scan(rh, hs(log_w), kh, vh, kk, hs(a))
    mean = jnp.mean(o, axis=-1, keepdims=True)
    var = jnp.mean(jnp.square(o - mean), axis=-1, keepdims=True)
    o = ((o - mean) * lax.rsqrt(var + RWKV_GN_EPS)).reshape(B, S, RW_W) * ln_w.astype(f32) + ln_b.astype(f32)
    bonus = jnp.sum(rh * kh * r_k.astype(f32), axis=-1, keepdims=True) * vh
    o_r = (o + bonus.reshape(B, S, RW_W)) * g
    y = jnp.concatenate([o_h, o_r], axis=-1)
    return dense(y, w_out)


LANES = 128
SUBLANES = 8
TOKENS_PER_VREG = SUBLANES * LANES


def _take_top(w_ref, n, count, emit, extra_ref=None):
    def one_round(i, prev):
        best = jnp.full((SUBLANES, LANES), -jnp.inf, jnp.float32)
        where = jnp.zeros((SUBLANES, LANES), jnp.int32)
        extra = jnp.zeros((SUBLANES, LANES), jnp.int32)
        for c in range(n):
            val = jnp.where(prev == c, -jnp.inf, w_ref[c])
            w_ref[c] = val
            better = val > best
            best = jnp.where(better, val, best)
            where = jnp.where(better, c, where)
            if extra_ref is not None:
                extra = jnp.where(better, extra_ref[c], extra)
        emit(i, best, where, extra)
        return where

    lax.fori_loop(0, count, one_round, jnp.full((SUBLANES, LANES), -1, jnp.int32))


def _peer_topk_body(s0_ref, s1_ref, idx_ref, gate_ref, w_ref, val_ref, pos_ref, cand_ref, cand_id_ref):
    nk = s0_ref.shape[0]
    topk = idx_ref.shape[0]
    for half, s_ref in enumerate((s0_ref, s1_ref)):
        w_ref[...] = s_ref[...]

        def emit_half(i, value, index, _, half=half):
            val_ref[half, i] = value
            pos_ref[half, i] = index

        _take_top(w_ref, nk, topk, emit_half)

    pairs = [(a, b) for a in range(topk) for b in range(topk) if (a + 1) * (b + 1) <= topk]
    for c, (a, b) in enumerate(pairs):
        cand_ref[c] = val_ref[0, a] + val_ref[1, b]
        cand_id_ref[c] = pos_ref[0, a] * nk + pos_ref[1, b]

    def emit_pair(i, value, _, expert):
        gate_ref[i] = value
        idx_ref[i] = expert

    _take_top(cand_ref, len(pairs), topk, emit_pair, extra_ref=cand_id_ref)

    best = gate_ref[...]
    e = jnp.exp(best - best[0:1])
    gate_ref[...] = e / jnp.sum(e, axis=0, keepdims=True)


def peer_topk(scores_t):
    H, _, NK, T = scores_t.shape
    K = PEER_TOPK
    G = T // TOKENS_PER_VREG
    npairs = sum(1 for a in range(K) for b in range(K) if (a + 1) * (b + 1) <= K)
    s = scores_t.reshape(H * 2, NK, G, SUBLANES, LANES)
    in_block = (None, NK, None, SUBLANES, LANES)
    out_block = (None, K, None, SUBLANES, LANES)
    idx_t, gate_t = pl.pallas_call(
        _peer_topk_body,
        grid=(H, G),
        in_specs=[pl.BlockSpec(in_block, lambda h, g: (2 * h, 0, g, 0, 0)),
                  pl.BlockSpec(in_block, lambda h, g: (2 * h + 1, 0, g, 0, 0))],
        out_specs=[pl.BlockSpec(out_block, lambda h, g: (h, 0, g, 0, 0)),
                   pl.BlockSpec(out_block, lambda h, g: (h, 0, g, 0, 0))],
        out_shape=[jax.ShapeDtypeStruct((H, K, G, SUBLANES, LANES), jnp.int32),
                   jax.ShapeDtypeStruct((H, K, G, SUBLANES, LANES), jnp.float32)],
        scratch_shapes=[pltpu.VMEM((NK, SUBLANES, LANES), jnp.float32),
                        pltpu.VMEM((2, K, SUBLANES, LANES), jnp.float32),
                        pltpu.VMEM((2, K, SUBLANES, LANES), jnp.int32),
                        pltpu.VMEM((npairs, SUBLANES, LANES), jnp.float32),
                        pltpu.VMEM((npairs, SUBLANES, LANES), jnp.int32)],
        compiler_params=pltpu.CompilerParams(dimension_semantics=("arbitrary", "arbitrary")),
        name="peer_topk",
    )(s, s)
    to_tokens = lambda t: t.transpose(2, 3, 4, 0, 1).reshape(T, H * K)
    return to_tokens(idx_t), to_tokens(gate_t)


def peer(hn, w_q, sub_keys, u, v):
    B, S, D = hn.shape
    T = B * S
    half = PEER_DKEY // 2
    x = hn.reshape(T, D)
    q = dense(x, w_q).reshape(T, PEER_HEADS, 2, half)
    scores_t = jnp.einsum('thpd,hpnd->hpnt', q, sub_keys.astype(jnp.float32))
    idx, gate = peer_topk(scores_t)
    out = peer_experts(x, idx, gate, u, v)
    return out.reshape(B, S, D)


PEER_TOKEN_BLOCK = 512
PEER_GROUP = SUBLANES
PEER_UNROLL = 32
PEER_DOT_DMAS = 20
PEER_AXPY_CHAINS = 4


def _peer_experts_body(idx_hbm, x_ref, g_ref, uv_hbm, o_ref, idx_smem, buf, r_scr, sb_scr, sem, idx_sem):
    tb, rows, _ = x_ref.shape
    npair = g_ref.shape[1]
    grp = buf.shape[0] // 2
    ngroups = tb // grp
    half = rows // 2
    nbody = npair // PEER_UNROLL
    dot_dmas, axpy_dmas = PEER_DOT_DMAS, PEER_UNROLL - PEER_DOT_DMAS
    dot_at = {(n * PEER_UNROLL) // dot_dmas: n for n in range(dot_dmas)}
    axpy_at = {(n * PEER_UNROLL) // axpy_dmas: n for n in range(axpy_dmas)}
    i = pl.program_id(0)

    idx_copy = pltpu.make_async_copy(idx_hbm.at[pl.ds(i * (tb * npair), tb * npair)], idx_smem, idx_sem)
    idx_copy.start()
    idx_copy.wait()

    def row_copy(tok, k, slot):
        e = idx_smem[tok * npair + k]
        return pltpu.make_async_copy(uv_hbm.at[e], buf.at[slot, k], sem.at[slot])

    def ahead_copy(tok, first, count, slot, n):
        e = idx_smem[tok * npair + first + n]
        return pltpu.make_async_copy(uv_hbm.at[e], buf.at[slot, pl.ds(first, count)].at[n], sem.at[slot])

    def slot_wait(slot):
        pltpu.make_async_copy(uv_hbm.at[pl.ds(0, npair)], buf.at[slot], sem.at[slot]).wait()

    for tok in range(grp):
        def issue_first(k, c, tok=tok):
            row_copy(tok, k, tok).start()
            return c
        lax.fori_loop(0, npair, issue_first, 0, unroll=PEER_UNROLL)

    lane = lax.broadcasted_iota(jnp.int32, (npair, LANES), 1)
    high_half = jnp.uint32(0xFFFF0000)

    def group_pair(gg, carry):
        for par in range(2):
            g = gg * 2 + par
            g_next = jnp.minimum(g + 1, ngroups - 1)
            cur, nxt = par * grp, (1 - par) * grp

            for tok in range(grp):
                t = g * grp + tok
                t_next = g_next * grp + tok
                slot_wait(cur + tok)
                x_lo = x_ref[t, pl.ds(0, half), :]
                x_hi = x_ref[t, pl.ds(half, half), :]

                def dot_body(j, c, tok=tok, t_next=t_next, x_lo=x_lo, x_hi=x_hi):
                    for q in range(PEER_UNROLL):
                        k = j * PEER_UNROLL + q
                        if q in dot_at:
                            ahead_copy(t_next, j * dot_dmas, dot_dmas, nxt + tok,
                                       dot_at[q]).start(priority=dot_at[q] % 2)
                        u_lo = pltpu.bitcast(buf[cur + tok, k, pl.ds(0, half), :] & high_half, jnp.float32)
                        u_hi = pltpu.bitcast(buf[cur + tok, k, pl.ds(half, half), :] & high_half, jnp.float32)
                        r_scr[tok, pl.ds(k, 1), :] = jnp.sum(x_lo * u_lo + x_hi * u_hi, axis=0, keepdims=True)
                    return c

                lax.fori_loop(0, nbody, dot_body, 0)

            d = jnp.zeros((npair, LANES), jnp.float32)
            for tok in range(grp):
                d = jnp.where(lane == tok, jnp.sum(r_scr[tok], axis=-1, keepdims=True), d)
            g_tile = g_ref[pl.ds(pl.multiple_of(g * grp, grp), grp), :]
            g_cols = jnp.concatenate([g_tile, jnp.zeros((npair - grp, npair), jnp.float32)], axis=0).T
            s = g_cols * (0.5 * d * (1.0 + lax.erf(d * (2.0 ** -0.5))))
            for tok in range(grp):
                sb_scr[tok] = jnp.broadcast_to(s[:, tok:tok + 1], (npair, LANES))

            for tok in range(grp):
                t = g * grp + tok
                t_next = g_next * grp + tok

                def axpy_body(j, acc, tok=tok, t_next=t_next):
                    acc = list(acc)
                    for q in range(PEER_UNROLL):
                        k = j * PEER_UNROLL + q
                        if q in axpy_at:
                            ahead_copy(t_next, nbody * dot_dmas + j * axpy_dmas, axpy_dmas, nxt + tok,
                                       axpy_at[q]).start(priority=axpy_at[q] % 2)
                        sk = sb_scr[tok, pl.ds(k, 1), :]
                        v_lo = pltpu.bitcast(buf[cur + tok, k, pl.ds(0, half), :] << 16, jnp.float32)
                        v_hi = pltpu.bitcast(buf[cur + tok, k, pl.ds(half, half), :] << 16, jnp.float32)
                        c = 2 * (q % PEER_AXPY_CHAINS)
                        acc[c] = acc[c] + sk * v_lo
                        acc[c + 1] = acc[c + 1] + sk * v_hi
                    return tuple(acc)

                zero = jnp.zeros((half, LANES), jnp.float32)
                acc = lax.fori_loop(0, nbody, axpy_body, (zero,) * (2 * PEER_AXPY_CHAINS))
                o_ref[t, pl.ds(0, half), :] = sum(acc[2::2], acc[0])
                o_ref[t, pl.ds(half, half), :] = sum(acc[3::2], acc[1])
        return carry

    lax.fori_loop(0, ngroups // 2, group_pair, 0)
    for tok in range(grp):
        slot_wait(tok)


def pack_expert_tables(u, v):
    E, D = u.shape
    hi = lax.bitcast_convert_type(u.astype(jnp.bfloat16), jnp.uint16).astype(jnp.uint32) << 16
    lo = lax.bitcast_convert_type(v.astype(jnp.bfloat16), jnp.uint16).astype(jnp.uint32)
    return (hi | lo).reshape(E, D // LANES, LANES)


def peer_experts(x, idx, gate, u, v):
    T, D = x.shape
    P = idx.shape[1]
    rows = D // LANES
    tb = PEER_TOKEN_BLOCK
    nslot = 2 * PEER_GROUP
    out = pl.pallas_call(
        _peer_experts_body,
        grid=(T // tb,),
        in_specs=[pl.BlockSpec(memory_space=pl.ANY),
                  pl.BlockSpec((tb, rows, LANES), lambda i: (i, 0, 0)),
                  pl.BlockSpec((tb, P), lambda i: (i, 0)),
                  pl.BlockSpec(memory_space=pl.ANY)],
        out_specs=pl.BlockSpec((tb, rows, LANES), lambda i: (i, 0, 0)),
        out_shape=jax.ShapeDtypeStruct((T, rows, LANES), jnp.float32),
        scratch_shapes=[pltpu.SMEM((tb * P,), jnp.int32),
                        pltpu.VMEM((nslot, P, rows, LANES), jnp.uint32),
                        pltpu.VMEM((PEER_GROUP, P, LANES), jnp.float32),
                        pltpu.VMEM((PEER_GROUP, P, LANES), jnp.float32),
                        pltpu.SemaphoreType.DMA((nslot,)),
                        pltpu.SemaphoreType.DMA(())],
        compiler_params=pltpu.CompilerParams(dimension_semantics=("arbitrary",)),
        name="peer_experts",
    )(idx.reshape(T * P), x.reshape(T, rows, LANES), gate, pack_expert_tables(u, v))
    return out.reshape(T, D)


def _final_rmsnorm_body(x_ref, g_ref, o_ref):
    x = x_ref[...]
    y = x * lax.rsqrt(jnp.mean(x * x, axis=-1, keepdims=True) + EPS)
    o_ref[...] = y * g_ref[...]


def final_rmsnorm(h, g):
    B, S, D = h.shape
    x = h.reshape(B * S, D)
    rows = 512
    out = pl.pallas_call(
        _final_rmsnorm_body,
        grid=(B * S // rows,),
        in_specs=[pl.BlockSpec((rows, D), lambda i: (i, 0)), pl.BlockSpec((1, D), lambda i: (0, 0))],
        out_specs=pl.BlockSpec((rows, D), lambda i: (i, 0)),
        out_shape=jax.ShapeDtypeStruct((B * S, D), jnp.float32),
        name="final_rmsnorm",
    )(x, g.reshape(1, D))
    return out.reshape(B, S, D)


def kernel(x, p, ev_w_in, ev_gla_gk_w2, ev_gla_gk_b, ev_gla_norm, ev_gdn_conv, ev_gdn_a_log,
           ev_gdn_dt_bias, ev_gdn_norm, ev_w_out, od_w_in, hgrn_lb_logits, od_hgrn_norm,
           od_rwkv_mu, od_rwkv_w0, od_rwkv_w2, od_rwkv_a0, od_rwkv_a2, od_rwkv_g2,
           od_rwkv_k_k, od_rwkv_k_a, od_rwkv_r_k, od_rwkv_ln_w, od_rwkv_ln_b, od_w_out,
           norm_mix, norm_ffn, norm_ple, peer_w_q, peer_sub_keys, peer_u, peer_v,
           ple_w_proj, ple_w_gate, norm_final):
    f32 = jnp.float32
    sm = jax.nn.softmax(hgrn_lb_logits.astype(f32), axis=0)
    lower_bounds = jnp.cumsum(sm, axis=0) - sm[0]
    h = x
    for i in range(DEPTH):
        j = i // 2
        hn = rmsnorm(h, norm_mix[i])
        if i % 2 == 0:
            y = even_mixer(hn, ev_w_in[j], ev_gla_gk_w2[j], ev_gla_gk_b[j], ev_gla_norm[j], ev_gdn_conv[j],
                           ev_gdn_a_log[j], ev_gdn_dt_bias[j], ev_gdn_norm[j], ev_w_out[j])
        else:
            y = odd_mixer(hn, od_w_in[j], lower_bounds[i], od_hgrn_norm[j], od_rwkv_mu[j], od_rwkv_w0[j],
                          od_rwkv_w2[j], od_rwkv_a0[j], od_rwkv_a2[j], od_rwkv_g2[j], od_rwkv_k_k[j],
                          od_rwkv_k_a[j], od_rwkv_r_k[j], od_rwkv_ln_w[j], od_rwkv_ln_b[j], od_w_out[j])
        h = h + y.astype(h.dtype)
        h = h + peer(rmsnorm(h, norm_ffn[i]), peer_w_q[i], peer_sub_keys[i], peer_u[i], peer_v[i]).astype(h.dtype)
        gate = jax.nn.sigmoid(dense(rmsnorm(h, norm_ple[i]), ple_w_gate[i]))
        h = h + gate * dense(p[i], ple_w_proj[i])
    return final_rmsnorm(h, norm_final)
```

```python
import math
import jax
import jax.numpy as jnp
from jax import lax
import numpy as np
from jax.experimental import pallas as pl
from jax.experimental.pallas import tpu as pltpu

D_MODEL = 2048
BATCH = 8
SEQ = 2048
DEPTH = 2

PLE_DIM = 256
CHUNK = 64
EPS = 1e-6

GLA_HEADS = 4
GLA_DV = D_MODEL // (2 * GLA_HEADS)
GLA_DK = GLA_DV // 2
GLA_GATE_RANK = 16
GLA_GATE_NORM = 16.0
GDN_DK = 128
GDN_DV = 128
GDN_HEADS = D_MODEL // (2 * GDN_DV)
CONV_K = 4
HGRN_DK = 128
HGRN_DV = 128
HGRN_HEADS = D_MODEL // (2 * HGRN_DV)
RWKV_HD = 64
RWKV_HEADS = D_MODEL // (2 * RWKV_HD)
RWKV_W_LORA = 96
RWKV_A_LORA = 96
RWKV_G_LORA = 256
RWKV_GN_EPS = 64e-5
PEER_HEADS = 8
PEER_DKEY = 256
PEER_NKEYS = 128
PEER_EXPERTS = PEER_NKEYS * PEER_NKEYS
PEER_TOPK = 16
PEER_BLOCK = 128

GLA_QK = GLA_HEADS * GLA_DK
GLA_VW = GLA_HEADS * GLA_DV
GDN_QK = GDN_HEADS * GDN_DK
GDN_VW = GDN_HEADS * GDN_DV
EVEN_COLS = (GLA_QK, GLA_QK, GLA_VW, GLA_VW, GLA_GATE_RANK, GDN_QK, GDN_QK, GDN_VW, GDN_VW, GDN_HEADS, GDN_HEADS)
EVEN_IN = sum(EVEN_COLS)
HG_W = HGRN_HEADS * HGRN_DK
HG_VW = HGRN_HEADS * HGRN_DV
RW_W = RWKV_HEADS * RWKV_HD
HGRN_COLS = (HG_W, HG_W, HG_VW, HG_VW)
RWKV_COLS = (RW_W, RW_W, RW_W, RWKV_W_LORA, RWKV_A_LORA, RWKV_G_LORA)
HGRN_IN = sum(HGRN_COLS)
RWKV_IN = sum(RWKV_COLS)
ODD_IN = HGRN_IN + RWKV_IN


DENSE_ROWS = 1024
DENSE_COLS = 512


def _dense_body(x_ref, w_ref, o_ref):
    o_ref[...] = jnp.dot(x_ref[...].astype(jnp.bfloat16), w_ref[...].astype(jnp.bfloat16),
                         preferred_element_type=jnp.float32)


def dense(x, w):
    lead, K = x.shape[:-1], x.shape[-1]
    N = w.shape[1]
    x2 = x.reshape(-1, K)
    M = x2.shape[0]
    tm, tn = min(DENSE_ROWS, M), min(DENSE_COLS, N)
    out = pl.pallas_call(
        _dense_body,
        grid=(pl.cdiv(M, tm), pl.cdiv(N, tn)),
        in_specs=[pl.BlockSpec((tm, K), lambda i, j: (i, 0)), pl.BlockSpec((K, tn), lambda i, j: (0, j))],
        out_specs=pl.BlockSpec((tm, tn), lambda i, j: (i, j)),
        out_shape=jax.ShapeDtypeStruct((M, N), jnp.float32),
        compiler_params=pltpu.CompilerParams(dimension_semantics=("arbitrary", "arbitrary")),
        name="dense",
    )(x2, w)
    return out.reshape(lead + (N,))


def rmsnorm(x, g):
    x32 = x.astype(jnp.float32)
    y = x32 * lax.rsqrt(jnp.mean(x32 * x32, axis=-1, keepdims=True) + EPS)
    return (y * g.astype(jnp.float32)).astype(x.dtype)


def gated_rmsnorm(o, z, g):
    o = o * lax.rsqrt(jnp.mean(o * o, axis=-1, keepdims=True) + EPS)
    return o * g.astype(jnp.float32) * jax.nn.silu(z)


def l2norm(t):
    return t * lax.rsqrt(jnp.sum(t * t, axis=-1, keepdims=True) + EPS)


def split_cols(t, sizes):
    return jnp.split(t, np.cumsum(sizes)[:-1].tolist(), axis=-1)


def to_heads(t, n_heads):
    B, S, _ = t.shape
    return t.reshape(B, S, n_heads, -1).transpose(0, 2, 1, 3)


def from_heads(t):
    B, H, S, d = t.shape
    return t.transpose(0, 2, 1, 3).reshape(B, S, H * d)


def causal_dwconv(x, w):
    return lax.conv_general_dilated(x, w[:, None, :], window_strides=(1,), padding=[(w.shape[0] - 1, 0)],
                                    dimension_numbers=('NWC', 'WIO', 'NWC'), feature_group_count=x.shape[-1])


def token_shift(t, mu):
    prev = jnp.pad(t, ((0, 0), (1, 0), (0, 0)))[:, :-1]
    return t + (prev - t) * mu


def _chunk_gla_body(q_ref, k_ref, v_ref, g_ref, z_ref, gain_ref, o_ref, state_ref, b_ref):
    C, dk = q_ref.shape
    f32 = jnp.float32

    @pl.when(pl.program_id(2) == 0)
    def _():
        state_ref[...] = jnp.zeros_like(state_ref)

    q = q_ref[...]
    v = v_ref[...]
    row = lax.broadcasted_iota(jnp.int32, (C, C), 0)
    col = lax.broadcasted_iota(jnp.int32, (C, C), 1)
    b = jnp.dot((row >= col).astype(f32), g_ref[...], precision=lax.Precision.HIGHEST,
                preferred_element_type=f32)
    b_ref[...] = b

    t_id = lax.broadcasted_iota(jnp.int32, (C, 1), 0)
    attn = jnp.zeros((C, C), f32)
    for s in range(C):
        top = (s // SUBLANES) * SUBLANES
        keep = t_id[top:] >= s
        diff = b[top:] - b_ref[pl.ds(s, 1), :]
        m = jnp.where(keep, q[top:] * k_ref[pl.ds(s, 1), :] * jnp.exp(jnp.where(keep, diff, 0.0)), 0.0)
        colv = jnp.sum(m, axis=-1, keepdims=True)
        if top:
            colv = jnp.concatenate([jnp.zeros((top, 1), f32), colv], axis=0)
        attn = jnp.where(col == s, colv, attn)

    state_t = state_ref[...]
    o = lax.dot_general(q * jnp.exp(b), state_t, (((1,), (1,)), ((), ())), preferred_element_type=f32)
    o = o + jnp.dot(attn, v, preferred_element_type=f32)
    b_last = b_ref[pl.ds(C - 1, 1), :]
    k_dec = k_ref[...] * jnp.exp(b_last - b)
    state_ref[...] = state_t * jnp.exp(b_last) + lax.dot_general(v, k_dec, (((0,), (0,)), ((), ())),
                                                                  preferred_element_type=f32)
    z = z_ref[...]
    o = o * lax.rsqrt(jnp.mean(o * o, axis=-1, keepdims=True) + EPS)
    o_ref[...] = o * gain_ref[...] * (z * jax.nn.sigmoid(z))


def chunk_gla_gated(q, k, v, log_g, z, gain, n_heads):
    B, S, _ = q.shape
    dk = q.shape[-1] // n_heads
    dv = v.shape[-1] // n_heads
    qk_spec = pl.BlockSpec((None, CHUNK, dk), lambda b, h, c: (b, c, h))
    v_spec = pl.BlockSpec((None, CHUNK, dv), lambda b, h, c: (b, c, h))
    return pl.pallas_call(
        _chunk_gla_body,
        grid=(B, n_heads, S // CHUNK),
        in_specs=[qk_spec, qk_spec, v_spec, qk_spec, v_spec, pl.BlockSpec((1, dv), lambda b, h, c: (0, 0))],
        out_specs=v_spec,
        out_shape=jax.ShapeDtypeStruct(v.shape, jnp.float32),
        scratch_shapes=[pltpu.VMEM((dv, dk), jnp.float32), pltpu.VMEM((CHUNK, dk), jnp.float32)],
        compiler_params=pltpu.CompilerParams(dimension_semantics=("arbitrary", "arbitrary", "arbitrary")),
        name="chunk_gla",
    )(q, k, v, log_g, z, gain.reshape(1, dv).astype(jnp.float32))


def _gated_delta_body(q_ref, k_ref, v_ref, z_ref, bcol_ref, brow_ref, beta_ref, gain_ref, o_ref, state_ref):
    f32 = jnp.float32
    n_heads, dk, dv = state_ref.shape
    C = q_ref.shape[0]
    exact = lax.Precision.HIGHEST
    nt = (((1,), (1,)), ((), ()))
    tn = (((0,), (0,)), ((), ()))

    @pl.when(pl.program_id(1) == 0)
    def _():
        state_ref[...] = jnp.zeros_like(state_ref)

    row = lax.broadcasted_iota(jnp.int32, (C, C), 0)
    col = lax.broadcasted_iota(jnp.int32, (C, C), 1)
    causal = row >= col
    strict = row > col
    eye = (row == col).astype(f32)
    heads = range(n_heads)
    head_k = lambda h: k_ref[:, h * dk:(h + 1) * dk]
    head_v = lambda h: v_ref[:, h * dv:(h + 1) * dv]
    decay = [jnp.where(causal, jnp.exp(jnp.where(causal, bcol_ref[h] - brow_ref[h], 0.0)), 0.0) for h in heads]
    kb = [head_k(h) * beta_ref[h] for h in heads]
    power = [-jnp.where(strict, lax.dot_general(kb[h], head_k(h), nt, preferred_element_type=f32) * decay[h], 0.0)
             for h in heads]
    inv = [eye + power[h] for h in heads]
    span = 1
    while 2 * span < C:
        power = [jnp.dot(power[h], power[h], precision=exact, preferred_element_type=f32) for h in heads]
        inv = [inv[h] + jnp.dot(inv[h], power[h], precision=exact, preferred_element_type=f32) for h in heads]
        span *= 2
    u = [jnp.dot(inv[h], head_v(h) * beta_ref[h], preferred_element_type=f32) for h in heads]
    w = [jnp.dot(inv[h], kb[h] * jnp.exp(bcol_ref[h]), preferred_element_type=f32) for h in heads]
    qk = [lax.dot_general(q_ref[:, h * dk:(h + 1) * dk], head_k(h), nt, preferred_element_type=f32) * decay[h]
          for h in heads]
    v_new = [u[h] - jnp.dot(w[h], state_ref[h], preferred_element_type=f32) for h in heads]
    for h in heads:
        bc = bcol_ref[h]
        state = state_ref[h]
        o = (jnp.dot(q_ref[:, h * dk:(h + 1) * dk] * jnp.exp(bc), state, preferred_element_type=f32)
             + jnp.dot(qk[h], v_new[h], preferred_element_type=f32))
        b_last = bc[C - 1:C, :]
        state_ref[h] = state * jnp.exp(b_last) + lax.dot_general(head_k(h) * jnp.exp(b_last - bc), v_new[h], tn,
                                                                 preferred_element_type=f32)
        z = z_ref[:, h * dv:(h + 1) * dv]
        o = o * lax.rsqrt(jnp.mean(o * o, axis=-1, keepdims=True) + EPS)
        o_ref[:, h * dv:(h + 1) * dv] = o * gain_ref[...] * (z * jax.nn.sigmoid(z))


def gated_delta_gated(q, k, v, log_a, beta, z, gain, n_heads):
    B, S, _ = q.shape
    dk = q.shape[-1] // n_heads
    dv = v.shape[-1] // n_heads
    n = S // CHUNK
    b = jnp.cumsum(log_a.astype(jnp.float32).transpose(0, 2, 1).reshape(B, n_heads, n, CHUNK), axis=-1)
    b_col = b.reshape(B, n_heads, S, 1)
    b_row = b.reshape(B, n_heads, n, 1, CHUNK)
    beta_col = beta.astype(jnp.float32).transpose(0, 2, 1).reshape(B, n_heads, S, 1)
    wide = lambda d: pl.BlockSpec((None, CHUNK, n_heads * d), lambda bi, c: (bi, c, 0))
    col_spec = pl.BlockSpec((None, n_heads, CHUNK, 1), lambda bi, c: (bi, 0, c, 0))
    return pl.pallas_call(
        _gated_delta_body,
        grid=(B, n),
        in_specs=[wide(dk), wide(dk), wide(dv), wide(dv), col_spec,
                  pl.BlockSpec((None, n_heads, None, 1, CHUNK), lambda bi, c: (bi, 0, c, 0, 0)),
                  col_spec, pl.BlockSpec((1, dv), lambda bi, c: (0, 0))],
        out_specs=wide(dv),
        out_shape=jax.ShapeDtypeStruct(v.shape, jnp.float32),
        scratch_shapes=[pltpu.VMEM((n_heads, dk, dv), jnp.float32)],
        compiler_params=pltpu.CompilerParams(dimension_semantics=("arbitrary", "arbitrary")),
        name="gated_delta",
    )(q, k, v, z, b_col, b_row, beta_col, gain.reshape(1, dv).astype(jnp.float32))


RWKV_TIME_BLOCK = 16


def _rwkv7_scan_body(r_ref, w_ref, k_ref, v_ref, kk_ref, a_ref, o_ref, s_ref, ew_ref, kka_ref):
    ts, n, _ = r_ref.shape

    @pl.when(pl.program_id(0) == 0)
    def _():
        s_ref[...] = jnp.zeros_like(s_ref)

    ew_ref[...] = jnp.exp(w_ref[...])
    kka_ref[...] = kk_ref[...] * a_ref[...]

    def step(t, carry):
        sa = s_ref[0] * kk_ref[t, pl.ds(0, 1), :]
        for j in range(1, n):
            sa = sa + s_ref[j] * kk_ref[t, pl.ds(j, 1), :]
        v_t = v_ref[t]
        o = None
        for j in range(n):
            s_new = (s_ref[j] * ew_ref[t, pl.ds(j, 1), :] - sa * kka_ref[t, pl.ds(j, 1), :]
                     + v_t * k_ref[t, pl.ds(j, 1), :])
            s_ref[j] = s_new
            term = s_new * r_ref[t, pl.ds(j, 1), :]
            o = term if o is None else o + term
        o_ref[t] = o
        return carry

    lax.fori_loop(0, ts, step, 0)


def rwkv7_scan(r, log_w, k, v, kk, a):
    B, S, H, N = r.shape
    L = B * H

    def to_lanes(t):
        return t.transpose(1, 3, 0, 2).reshape(S, N, L)

    ts = RWKV_TIME_BLOCK
    spec = pl.BlockSpec((ts, N, L), lambda i: (i, 0, 0))
    o = pl.pallas_call(
        _rwkv7_scan_body,
        grid=(S // ts,),
        in_specs=[spec] * 6,
        out_specs=spec,
        out_shape=jax.ShapeDtypeStruct((S, N, L), jnp.float32),
        scratch_shapes=[pltpu.VMEM((N, N, L), jnp.float32),
                        pltpu.VMEM((ts, N, L), jnp.float32),
                        pltpu.VMEM((ts, N, L), jnp.float32)],
        compiler_params=pltpu.CompilerParams(dimension_semantics=("arbitrary",)),
        name="rwkv7_scan",
    )(*(to_lanes(t) for t in (r, log_w, k, v, kk, a)))
    return o.reshape(S, N, B, H).transpose(2, 0, 3, 1)


def even_mixer(hn, w_in, gk_w2, gk_b, gla_norm, conv_w, a_log, dt_bias, gdn_norm, w_out):
    f32 = jnp.float32
    proj = dense(hn, w_in)
    gq, gk, gv, gz, glr, bq, bk, bv, bz, ba, bb = split_cols(proj, EVEN_COLS)
    log_g = jax.nn.log_sigmoid(dense(glr, gk_w2) + gk_b.astype(f32)) / GLA_GATE_NORM
    o_a = chunk_gla_gated(gq * GLA_DK ** -0.5, gk, gv, log_g, gz, gla_norm, GLA_HEADS)
    qkv = jax.nn.silu(causal_dwconv(jnp.concatenate([bq, bk, bv], axis=-1), conv_w.astype(f32)))
    cq, ck, cv = split_cols(qkv, (GDN_QK, GDN_QK, GDN_VW))
    log_a = -jnp.exp(a_log.astype(f32)) * jax.nn.softplus(ba + dt_bias.astype(f32))
    beta = jax.nn.sigmoid(bb)

    def head_l2norm(t):
        return l2norm(t.reshape(t.shape[:2] + (GDN_HEADS, GDN_DK))).reshape(t.shape)

    o_b = gated_delta_gated(head_l2norm(cq) * GDN_DK ** -0.5, head_l2norm(ck), cv, log_a, beta, bz, gdn_norm,
                            GDN_HEADS)
    y = jnp.concatenate([o_a, o_b], axis=-1)
    return dense(y, w_out)


def odd_mixer(hn, w_in, lb, hg_norm, mu, w0, w2, a0, a2, g2, k_k, k_a, r_k, ln_w, ln_b, w_out):
    f32 = jnp.float32
    B, S, _ = hn.shape
    proj = dense(hn, w_in)
    hq, hf, hi, hg = split_cols(proj[..., :HGRN_IN], HGRN_COLS)
    lb = lb.astype(f32)
    fgate = lb + (1.0 - lb) * jax.nn.sigmoid(hf)
    o_h = chunk_gla_gated(jax.nn.silu(hq), 1.0 - fgate, hi, jnp.log(fgate), hg, hg_norm, HGRN_HEADS)

    def hs(t):
        return t.reshape(B, S, RWKV_HEADS, RWKV_HD)
    r, k, v, w_lr, a_lr, g_lr = split_cols(token_shift(proj[..., HGRN_IN:], mu.astype(f32)), RWKV_COLS)
    log_w = -jnp.exp(-jax.nn.softplus(-(w0.astype(f32) + dense(jnp.tanh(w_lr), w2))) - 0.5)
    a = jax.nn.sigmoid(a0.astype(f32) + dense(a_lr, a2))
    g = dense(jax.nn.sigmoid(g_lr), g2)
    kk = l2norm(hs(k * k_k.astype(f32)))
    k = k * (1.0 + (a - 1.0) * k_a.astype(f32))
    rh, kh, vh = hs(r), hs(k), hs(v)
    o = rwkv7_scan(rh, hs(log_w), kh, vh, kk, hs(a))
    mean = jnp.mean(o, axis=-1, keepdims=True)
    var = jnp.mean(jnp.square(o - mean), axis=-1, keepdims=True)
    o = ((o - mean) * lax.rsqrt(var + RWKV_GN_EPS)).reshape(B, S, RW_W) * ln_w.astype(f32) + ln_b.astype(f32)
    bonus = jnp.sum(rh * kh * r_k.astype(f32), axis=-1, keepdims=True) * vh
    o_r = (o + bonus.reshape(B, S, RW_W)) * g
    y = jnp.concatenate([o_h, o_r], axis=-1)
    return dense(y, w_out)


LANES = 128
SUBLANES = 8
TOKENS_PER_VREG = SUBLANES * LANES


def _take_top(w_ref, n, count, emit, extra_ref=None):
    def one_round(i, prev):
        best = jnp.full((SUBLANES, LANES), -jnp.inf, jnp.float32)
        where = jnp.zeros((SUBLANES, LANES), jnp.int32)
        extra = jnp.zeros((SUBLANES, LANES), jnp.int32)
        for c in range(n):
            val = jnp.where(prev == c, -jnp.inf, w_ref[c])
            w_ref[c] = val
            better = val > best
            best = jnp.where(better, val, best)
            where = jnp.where(better, c, where)
            if extra_ref is not None:
                extra = jnp.where(better, extra_ref[c], extra)
        emit(i, best, where, extra)
        return where

    lax.fori_loop(0, count, one_round, jnp.full((SUBLANES, LANES), -1, jnp.int32))


def _peer_topk_body(s0_ref, s1_ref, idx_ref, gate_ref, w_ref, val_ref, pos_ref, cand_ref, cand_id_ref):
    nk = s0_ref.shape[0]
    topk = idx_ref.shape[0]
    for half, s_ref in enumerate((s0_ref, s1_ref)):
        w_ref[...] = s_ref[...]

        def emit_half(i, value, index, _, half=half):
            val_ref[half, i] = value
            pos_ref[half, i] = index

        _take_top(w_ref, nk, topk, emit_half)

    pairs = [(a, b) for a in range(topk) for b in range(topk) if (a + 1) * (b + 1) <= topk]
    for c, (a, b) in enumerate(pairs):
        cand_ref[c] = val_ref[0, a] + val_ref[1, b]
        cand_id_ref[c] = pos_ref[0, a] * nk + pos_ref[1, b]

    def emit_pair(i, value, _, expert):
        gate_ref[i] = value
        idx_ref[i] = expert

    _take_top(cand_ref, len(pairs), topk, emit_pair, extra_ref=cand_id_ref)

    best = gate_ref[...]
    e = jnp.exp(best - best[0:1])
    gate_ref[...] = e / jnp.sum(e, axis=0, keepdims=True)


def peer_topk(scores_t):
    H, _, NK, T = scores_t.shape
    K = PEER_TOPK
    G = T // TOKENS_PER_VREG
    npairs = sum(1 for a in range(K) for b in range(K) if (a + 1) * (b + 1) <= K)
    s = scores_t.reshape(H * 2, NK, G, SUBLANES, LANES)
    in_block = (None, NK, None, SUBLANES, LANES)
    out_block = (None, K, None, SUBLANES, LANES)
    idx_t, gate_t = pl.pallas_call(
        _peer_topk_body,
        grid=(H, G),
        in_specs=[pl.BlockSpec(in_block, lambda h, g: (2 * h, 0, g, 0, 0)),
                  pl.BlockSpec(in_block, lambda h, g: (2 * h + 1, 0, g, 0, 0))],
        out_specs=[pl.BlockSpec(out_block, lambda h, g: (h, 0, g, 0, 0)),
                   pl.BlockSpec(out_block, lambda h, g: (h, 0, g, 0, 0))],
        out_shape=[jax.ShapeDtypeStruct((H, K, G, SUBLANES, LANES), jnp.int32),
                   jax.ShapeDtypeStruct((H, K, G, SUBLANES, LANES), jnp.float32)],
        scratch_shapes=[pltpu.VMEM((NK, SUBLANES, LANES), jnp.float32),
                        pltpu.VMEM((2, K, SUBLANES, LANES), jnp.float32),
                        pltpu.VMEM((2, K, SUBLANES, LANES), jnp.int32),
                        pltpu.VMEM((npairs, SUBLANES, LANES), jnp.float32),
                        pltpu.VMEM((npairs, SUBLANES, LANES), jnp.int32)],
        compiler_params=pltpu.CompilerParams(dimension_semantics=("arbitrary", "arbitrary")),
        name="peer_topk",
    )(s, s)
    to_tokens = lambda t: t.transpose(2, 3, 4, 0, 1).reshape(T, H * K)
    return to_tokens(idx_t), to_tokens(gate_t)


def peer(hn, w_q, sub_keys, u, v):
    B, S, D = hn.shape
    T = B * S
    half = PEER_DKEY // 2
    x = hn.reshape(T, D)
    q = dense(x, w_q).reshape(T, PEER_HEADS, 2, half)
    scores_t = jnp.einsum('thpd,hpnd->hpnt', q, sub_keys.astype(jnp.float32))
    idx, gate = peer_topk(scores_t)
    out = peer_experts(x, idx, gate, u, v)
    return out.reshape(B, S, D)


PEER_TOKEN_BLOCK = 512
PEER_GROUP = SUBLANES
PEER_UNROLL = 32
PEER_DOT_DMAS = 20
PEER_AXPY_CHAINS = 4


def _peer_experts_body(idx_hbm, x_ref, g_ref, uv_hbm, o_ref, idx_smem, buf, r_scr, sb_scr, sem, idx_sem):
    tb, rows, _ = x_ref.shape
    npair = g_ref.shape[1]
    grp = buf.shape[0] // 2
    ngroups = tb // grp
    half = rows // 2
    nbody = npair // PEER_UNROLL
    dot_dmas, axpy_dmas = PEER_DOT_DMAS, PEER_UNROLL - PEER_DOT_DMAS
    dot_at = {(n * PEER_UNROLL) // dot_dmas: n for n in range(dot_dmas)}
    axpy_at = {(n * PEER_UNROLL) // axpy_dmas: n for n in range(axpy_dmas)}
    i = pl.program_id(0)

    idx_copy = pltpu.make_async_copy(idx_hbm.at[pl.ds(i * (tb * npair), tb * npair)], idx_smem, idx_sem)
    idx_copy.start()
    idx_copy.wait()

    def row_copy(tok, k, slot):
        e = idx_smem[tok * npair + k]
        return pltpu.make_async_copy(uv_hbm.at[e], buf.at[slot, k], sem.at[slot])

    def ahead_copy(tok, first, count, slot, n):
        e = idx_smem[tok * npair + first + n]
        return pltpu.make_async_copy(uv_hbm.at[e], buf.at[slot, pl.ds(first, count)].at[n], sem.at[slot])

    def slot_wait(slot):
        pltpu.make_async_copy(uv_hbm.at[pl.ds(0, npair)], buf.at[slot], sem.at[slot]).wait()

    for tok in range(grp):
        def issue_first(k, c, tok=tok):
            row_copy(tok, k, tok).start()
            return c
        lax.fori_loop(0, npair, issue_first, 0, unroll=PEER_UNROLL)

    lane = lax.broadcasted_iota(jnp.int32, (npair, LANES), 1)
    high_half = jnp.uint32(0xFFFF0000)

    def group_pair(gg, carry):
        for par in range(2):
            g = gg * 2 + par
            g_next = jnp.minimum(g + 1, ngroups - 1)
            cur, nxt = par * grp, (1 - par) * grp

            for tok in range(grp):
                t = g * grp + tok
                t_next = g_next * grp + tok
                slot_wait(cur + tok)
                x_lo = x_ref[t, pl.ds(0, half), :]
                x_hi = x_ref[t, pl.ds(half, half), :]

                def dot_body(j, c, tok=tok, t_next=t_next, x_lo=x_lo, x_hi=x_hi):
                    for q in range(PEER_UNROLL):
                        k = j * PEER_UNROLL + q
                        if q in dot_at:
                            ahead_copy(t_next, j * dot_dmas, dot_dmas, nxt + tok,
                                       dot_at[q]).start(priority=dot_at[q] % 2)
                        u_lo = pltpu.bitcast(buf[cur + tok, k, pl.ds(0, half), :] & high_half, jnp.float32)
                        u_hi = pltpu.bitcast(buf[cur + tok, k, pl.ds(half, half), :] & high_half, jnp.float32)
                        r_scr[tok, pl.ds(k, 1), :] = jnp.sum(x_lo * u_lo + x_hi * u_hi, axis=0, keepdims=True)
                    return c

                lax.fori_loop(0, nbody, dot_body, 0)

            d = jnp.zeros((npair, LANES), jnp.float32)
            for tok in range(grp):
                d = jnp.where(lane == tok, jnp.sum(r_scr[tok], axis=-1, keepdims=True), d)
            g_tile = g_ref[pl.ds(pl.multiple_of(g * grp, grp), grp), :]
            g_cols = jnp.concatenate([g_tile, jnp.zeros((npair - grp, npair), jnp.float32)], axis=0).T
            s = g_cols * (0.5 * d * (1.0 + lax.erf(d * (2.0 ** -0.5))))
            for tok in range(grp):
                sb_scr[tok] = jnp.broadcast_to(s[:, tok:tok + 1], (npair, LANES))

            for tok in range(grp):
                t = g * grp + tok
                t_next = g_next * grp + tok

                def axpy_body(j, acc, tok=tok, t_next=t_next):
                    acc = list(acc)
                    for q in range(PEER_UNROLL):
                        k = j * PEER_UNROLL + q
                        if q in axpy_at:
                            ahead_copy(t_next, nbody * dot_dmas + j * axpy_dmas, axpy_dmas, nxt + tok,
                                       axpy_at[q]).start(priority=axpy_at[q] % 2)
                        sk = sb_scr[tok, pl.ds(k, 1), :]
                        v_lo = pltpu.bitcast(buf[cur + tok, k, pl.ds(0, half), :] << 16, jnp.float32)
                        v_hi = pltpu.bitcast(buf[cur + tok, k, pl.ds(half, half), :] << 16, jnp.float32)
                        c = 2 * (q % PEER_AXPY_CHAINS)
                        acc[c] = acc[c] + sk * v_lo
                        acc[c + 1] = acc[c + 1] + sk * v_hi
                    return tuple(acc)

                zero = jnp.zeros((half, LANES), jnp.float32)
                acc = lax.fori_loop(0, nbody, axpy_body, (zero,) * (2 * PEER_AXPY_CHAINS))
                o_ref[t, pl.ds(0, half), :] = sum(acc[2::2], acc[0])
                o_ref[t, pl.ds(half, half), :] = sum(acc[3::2], acc[1])
        return carry

    lax.fori_loop(0, ngroups // 2, group_pair, 0)
    for tok in range(grp):
        slot_wait(tok)


def pack_expert_tables(u, v):
    E, D = u.shape
    hi = lax.bitcast_convert_type(u.astype(jnp.bfloat16), jnp.uint16).astype(jnp.uint32) << 16
    lo = lax.bitcast_convert_type(v.astype(jnp.bfloat16), jnp.uint16).astype(jnp.uint32)
    return (hi | lo).reshape(E, D // LANES, LANES)


def peer_experts(x, idx, gate, u, v):
    T, D = x.shape
    P = idx.shape[1]
    rows = D // LANES
    tb = PEER_TOKEN_BLOCK
    nslot = 2 * PEER_GROUP
    out = pl.pallas_call(
        _peer_experts_body,
        grid=(T // tb,),
        in_specs=[pl.BlockSpec(memory_space=pl.ANY),
                  pl.BlockSpec((tb, rows, LANES), lambda i: (i, 0, 0)),
                  pl.BlockSpec((tb, P), lambda i: (i, 0)),
                  pl.BlockSpec(memory_space=pl.ANY)],
        out_specs=pl.BlockSpec((tb, rows, LANES), lambda i: (i, 0, 0)),
        out_shape=jax.ShapeDtypeStruct((T, rows, LANES), jnp.float32),
        scratch_shapes=[pltpu.SMEM((tb * P,), jnp.int32),
                        pltpu.VMEM((nslot, P, rows, LANES), jnp.uint32),
                        pltpu.VMEM((PEER_GROUP, P, LANES), jnp.float32),
                        pltpu.VMEM((PEER_GROUP, P, LANES), jnp.float32),
                        pltpu.SemaphoreType.DMA((nslot,)),
                        pltpu.SemaphoreType.DMA(())],
        compiler_params=pltpu.CompilerParams(dimension_semantics=("arbitrary",)),
        name="peer_experts",
    )(idx.reshape(T * P), x.reshape(T, rows, LANES), gate, pack_expert_tables(u, v))
    return out.reshape(T, D)


def _final_rmsnorm_body(x_ref, g_ref, o_ref):
    x = x_ref[...]
    y = x * lax.rsqrt(jnp.mean(x * x, axis=-1, keepdims=True) + EPS)
    o_ref[...] = y * g_ref[...]


def final_rmsnorm(h, g):
    B, S, D = h.shape
    x = h.reshape(B * S, D)
    rows = 512
    out = pl.pallas_call(
        _final_rmsnorm_body,
        grid=(B * S // rows,),
        in_specs=[pl.BlockSpec((rows, D), lambda i: (i, 0)), pl.BlockSpec((1, D), lambda i: (0, 0))],
        out_specs=pl.BlockSpec((rows, D), lambda i: (i, 0)),
        out_shape=jax.ShapeDtypeStruct((B * S, D), jnp.float32),
        name="final_rmsnorm",
    )(x, g.reshape(1, D))
    return out.reshape(B, S, D)


def kernel(x, p, ev_w_in, ev_gla_gk_w2, ev_gla_gk_b, ev_gla_norm, ev_gdn_conv, ev_gdn_a_log,
           ev_gdn_dt_bias, ev_gdn_norm, ev_w_out, od_w_in, hgrn_lb_logits, od_hgrn_norm,
           od_rwkv_mu, od_rwkv_w0, od_rwkv_w2, od_rwkv_a0, od_rwkv_a2, od_rwkv_g2,
           od_rwkv_k_k, od_rwkv_k_a, od_rwkv_r_k, od_rwkv_ln_w, od_rwkv_ln_b, od_w_out,
           norm_mix, norm_ffn, norm_ple, peer_w_q, peer_sub_keys, peer_u, peer_v,
           ple_w_proj, ple_w_gate, norm_final):
    f32 = jnp.float32
    sm = jax.nn.softmax(hgrn_lb_logits.astype(f32), axis=0)
    lower_bounds = jnp.cumsum(sm, axis=0) - sm[0]
    h = x
    for i in range(DEPTH):
        j = i // 2
        hn = rmsnorm(h, norm_mix[i])
        if i % 2 == 0:
            y = even_mixer(hn, ev_w_in[j], ev_gla_gk_w2[j], ev_gla_gk_b[j], ev_gla_norm[j], ev_gdn_conv[j],
                           ev_gdn_a_log[j], ev_gdn_dt_bias[j], ev_gdn_norm[j], ev_w_out[j])
        else:
            y = odd_mixer(hn, od_w_in[j], lower_bounds[i], od_hgrn_norm[j], od_rwkv_mu[j], od_rwkv_w0[j],
                          od_rwkv_w2[j], od_rwkv_a0[j], od_rwkv_a2[j], od_rwkv_g2[j], od_rwkv_k_k[j],
                          od_rwkv_k_a[j], od_rwkv_r_k[j], od_rwkv_ln_w[j], od_rwkv_ln_b[j], od_w_out[j])
        h = h + y.astype(h.dtype)
        h = h + peer(rmsnorm(h, norm_ffn[i]), peer_w_q[i], peer_sub_keys[i], peer_u[i], peer_v[i]).astype(h.dtype)
        gate = jax.nn.sigmoid(dense(rmsnorm(h, norm_ple[i]), ple_w_gate[i]))
        h = h + gate * dense(p[i], ple_w_proj[i])
    return final_rmsnorm(h, norm_final)
```

```python
import math
import jax
import jax.numpy as jnp
from jax import lax
import numpy as np
from jax.experimental import pallas as pl
from jax.experimental.pallas import tpu as pltpu

D_MODEL = 2048
BATCH = 8
SEQ = 2048
DEPTH = 2

PLE_DIM = 256
CHUNK = 64
EPS = 1e-6

GLA_HEADS = 4
GLA_DV = D_MODEL // (2 * GLA_HEADS)
GLA_DK = GLA_DV // 2
GLA_GATE_RANK = 16
GLA_GATE_NORM = 16.0
GDN_DK = 128
GDN_DV = 128
GDN_HEADS = D_MODEL // (2 * GDN_DV)
CONV_K = 4
HGRN_DK = 128
HGRN_DV = 128
HGRN_HEADS = D_MODEL // (2 * HGRN_DV)
RWKV_HD = 64
RWKV_HEADS = D_MODEL // (2 * RWKV_HD)
RWKV_W_LORA = 96
RWKV_A_LORA = 96
RWKV_G_LORA = 256
RWKV_GN_EPS = 64e-5
PEER_HEADS = 8
PEER_DKEY = 256
PEER_NKEYS = 128
PEER_EXPERTS = PEER_NKEYS * PEER_NKEYS
PEER_TOPK = 16
PEER_BLOCK = 128

GLA_QK = GLA_HEADS * GLA_DK
GLA_VW = GLA_HEADS * GLA_DV
GDN_QK = GDN_HEADS * GDN_DK
GDN_VW = GDN_HEADS * GDN_DV
EVEN_COLS = (GLA_QK, GLA_QK, GLA_VW, GLA_VW, GLA_GATE_RANK, GDN_QK, GDN_QK, GDN_VW, GDN_VW, GDN_HEADS, GDN_HEADS)
EVEN_IN = sum(EVEN_COLS)
HG_W = HGRN_HEADS * HGRN_DK
HG_VW = HGRN_HEADS * HGRN_DV
RW_W = RWKV_HEADS * RWKV_HD
HGRN_COLS = (HG_W, HG_W, HG_VW, HG_VW)
RWKV_COLS = (RW_W, RW_W, RW_W, RWKV_W_LORA, RWKV_A_LORA, RWKV_G_LORA)
HGRN_IN = sum(HGRN_COLS)
RWKV_IN = sum(RWKV_COLS)
ODD_IN = HGRN_IN + RWKV_IN


DENSE_ROWS = 1024
DENSE_COLS = 512


def _dense_body(x_ref, w_ref, o_ref, xb_ref):
    @pl.when(pl.program_id(1) == 0)
    def _():
        xb_ref[...] = x_ref[...].astype(jnp.bfloat16)

    o_ref[...] = jnp.dot(xb_ref[...], w_ref[...].astype(jnp.bfloat16), preferred_element_type=jnp.float32)


def _norm_dense_body(x_ref, g_ref, w_ref, o_ref, *rest):
    xb_ref = rest[-1]

    @pl.when(pl.program_id(1) == 0)
    def _():
        x = x_ref[...]
        xn = x * lax.rsqrt(jnp.mean(x * x, axis=-1, keepdims=True) + EPS) * g_ref[...]
        if len(rest) == 2:
            rest[0][...] = xn
        xb_ref[...] = xn.astype(jnp.bfloat16)

    o_ref[...] = jnp.dot(xb_ref[...], w_ref[...].astype(jnp.bfloat16), preferred_element_type=jnp.float32)


def dense(x, w, gain=None, return_normed=False):
    lead, K = x.shape[:-1], x.shape[-1]
    N = w.shape[1]
    x2 = x.reshape(-1, K).astype(jnp.float32)
    M = x2.shape[0]
    tm = min(DENSE_ROWS // 2 if return_normed else DENSE_ROWS, M)
    tn = min(DENSE_COLS, N)
    x_spec = pl.BlockSpec((tm, K), lambda i, j: (i, 0))
    w_spec = pl.BlockSpec((K, tn), lambda i, j: (0, j))
    o_spec = pl.BlockSpec((tm, tn), lambda i, j: (i, j))
    o_shape = jax.ShapeDtypeStruct((M, N), jnp.float32)
    params = dict(grid=(pl.cdiv(M, tm), pl.cdiv(N, tn)),
                  scratch_shapes=[pltpu.VMEM((tm, K), jnp.bfloat16)],
                  compiler_params=pltpu.CompilerParams(dimension_semantics=("arbitrary", "arbitrary")))
    if gain is None:
        out = pl.pallas_call(_dense_body, in_specs=[x_spec, w_spec], out_specs=o_spec, out_shape=o_shape,
                             name="dense", **params)(x2, w)
        return out.reshape(lead + (N,))
    norm_specs = [x_spec, pl.BlockSpec((1, K), lambda i, j: (0, 0)), w_spec]
    g2 = gain.reshape(1, K).astype(jnp.float32)
    if not return_normed:
        out = pl.pallas_call(_norm_dense_body, in_specs=norm_specs, out_specs=o_spec, out_shape=o_shape,
                             name="norm_dense", **params)(x2, g2, w)
        return out.reshape(lead + (N,))
    out, xn = pl.pallas_call(
        _norm_dense_body, in_specs=norm_specs, out_specs=[o_spec, x_spec],
        out_shape=[o_shape, jax.ShapeDtypeStruct((M, K), jnp.float32)], name="norm_dense_keep", **params)(x2, g2, w)
    return out.reshape(lead + (N,)), xn.reshape(lead + (K,))


def rmsnorm(x, g):
    x32 = x.astype(jnp.float32)
    y = x32 * lax.rsqrt(jnp.mean(x32 * x32, axis=-1, keepdims=True) + EPS)
    return (y * g.astype(jnp.float32)).astype(x.dtype)


def gated_rmsnorm(o, z, g):
    o = o * lax.rsqrt(jnp.mean(o * o, axis=-1, keepdims=True) + EPS)
    return o * g.astype(jnp.float32) * jax.nn.silu(z)


def l2norm(t):
    return t * lax.rsqrt(jnp.sum(t * t, axis=-1, keepdims=True) + EPS)


def split_cols(t, sizes):
    return jnp.split(t, np.cumsum(sizes)[:-1].tolist(), axis=-1)


def to_heads(t, n_heads):
    B, S, _ = t.shape
    return t.reshape(B, S, n_heads, -1).transpose(0, 2, 1, 3)


def from_heads(t):
    B, H, S, d = t.shape
    return t.transpose(0, 2, 1, 3).reshape(B, S, H * d)


def causal_dwconv(x, w):
    return lax.conv_general_dilated(x, w[:, None, :], window_strides=(1,), padding=[(w.shape[0] - 1, 0)],
                                    dimension_numbers=('NWC', 'WIO', 'NWC'), feature_group_count=x.shape[-1])


def token_shift(t, mu):
    prev = jnp.pad(t, ((0, 0), (1, 0), (0, 0)))[:, :-1]
    return t + (prev - t) * mu


def _chunk_gla_body(q_ref, k_ref, v_ref, g_ref, z_ref, gain_ref, o_ref, state_ref, b_ref):
    C, dk = q_ref.shape
    f32 = jnp.float32

    @pl.when(pl.program_id(2) == 0)
    def _():
        state_ref[...] = jnp.zeros_like(state_ref)

    q = q_ref[...]
    v = v_ref[...]
    row = lax.broadcasted_iota(jnp.int32, (C, C), 0)
    col = lax.broadcasted_iota(jnp.int32, (C, C), 1)
    b = jnp.dot((row >= col).astype(f32), g_ref[...], precision=lax.Precision.HIGHEST,
                preferred_element_type=f32)
    b_ref[...] = b

    t_id = lax.broadcasted_iota(jnp.int32, (C, 1), 0)
    attn = jnp.zeros((C, C), f32)
    for s in range(C):
        top = (s // SUBLANES) * SUBLANES
        keep = t_id[top:] >= s
        diff = b[top:] - b_ref[pl.ds(s, 1), :]
        m = jnp.where(keep, q[top:] * k_ref[pl.ds(s, 1), :] * jnp.exp(jnp.where(keep, diff, 0.0)), 0.0)
        colv = jnp.sum(m, axis=-1, keepdims=True)
        if top:
            colv = jnp.concatenate([jnp.zeros((top, 1), f32), colv], axis=0)
        attn = jnp.where(col == s, colv, attn)

    state_t = state_ref[...]
    o = lax.dot_general(q * jnp.exp(b), state_t, (((1,), (1,)), ((), ())), preferred_element_type=f32)
    o = o + jnp.dot(attn, v, preferred_element_type=f32)
    b_last = b_ref[pl.ds(C - 1, 1), :]
    k_dec = k_ref[...] * jnp.exp(b_last - b)
    state_ref[...] = state_t * jnp.exp(b_last) + lax.dot_general(v, k_dec, (((0,), (0,)), ((), ())),
                                                                  preferred_element_type=f32)
    z = z_ref[...]
    o = o * lax.rsqrt(jnp.mean(o * o, axis=-1, keepdims=True) + EPS)
    o_ref[...] = o * gain_ref[...] * (z * jax.nn.sigmoid(z))


def chunk_gla_gated(q, k, v, log_g, z, gain, n_heads):
    B, S, _ = q.shape
    dk = q.shape[-1] // n_heads
    dv = v.shape[-1] // n_heads
    qk_spec = pl.BlockSpec((None, CHUNK, dk), lambda b, h, c: (b, c, h))
    v_spec = pl.BlockSpec((None, CHUNK, dv), lambda b, h, c: (b, c, h))
    return pl.pallas_call(
        _chunk_gla_body,
        grid=(B, n_heads, S // CHUNK),
        in_specs=[qk_spec, qk_spec, v_spec, qk_spec, v_spec, pl.BlockSpec((1, dv), lambda b, h, c: (0, 0))],
        out_specs=v_spec,
        out_shape=jax.ShapeDtypeStruct(v.shape, jnp.float32),
        scratch_shapes=[pltpu.VMEM((dv, dk), jnp.float32), pltpu.VMEM((CHUNK, dk), jnp.float32)],
        compiler_params=pltpu.CompilerParams(dimension_semantics=("arbitrary", "arbitrary", "arbitrary")),
        name="chunk_gla",
    )(q, k, v, log_g, z, gain.reshape(1, dv).astype(jnp.float32))


def _gated_delta_body(q_ref, k_ref, v_ref, z_ref, bcol_ref, brow_ref, beta_ref, gain_ref, o_ref, state_ref):
    f32 = jnp.float32
    n_heads, dk, dv = state_ref.shape
    C = q_ref.shape[0]
    exact = lax.Precision.HIGHEST
    nt = (((1,), (1,)), ((), ()))
    tn = (((0,), (0,)), ((), ()))

    @pl.when(pl.program_id(1) == 0)
    def _():
        state_ref[...] = jnp.zeros_like(state_ref)

    row = lax.broadcasted_iota(jnp.int32, (C, C), 0)
    col = lax.broadcasted_iota(jnp.int32, (C, C), 1)
    causal = row >= col
    strict = row > col
    eye = (row == col).astype(f32)
    heads = range(n_heads)
    head_k = lambda h: k_ref[:, h * dk:(h + 1) * dk]
    head_v = lambda h: v_ref[:, h * dv:(h + 1) * dv]
    decay = [jnp.where(causal, jnp.exp(jnp.where(causal, bcol_ref[h] - brow_ref[h], 0.0)), 0.0) for h in heads]
    kb = [head_k(h) * beta_ref[h] for h in heads]
    power = [-jnp.where(strict, lax.dot_general(kb[h], head_k(h), nt, preferred_element_type=f32) * decay[h], 0.0)
             for h in heads]
    inv = [eye + power[h] for h in heads]
    span = 1
    while 2 * span < C:
        power = [jnp.dot(power[h], power[h], precision=exact, preferred_element_type=f32) for h in heads]
        inv = [inv[h] + jnp.dot(inv[h], power[h], precision=exact, preferred_element_type=f32) for h in heads]
        span *= 2
    u = [jnp.dot(inv[h], head_v(h) * beta_ref[h], preferred_element_type=f32) for h in heads]
    w = [jnp.dot(inv[h], kb[h] * jnp.exp(bcol_ref[h]), preferred_element_type=f32) for h in heads]
    qk = [lax.dot_general(q_ref[:, h * dk:(h + 1) * dk], head_k(h), nt, preferred_element_type=f32) * decay[h]
          for h in heads]
    v_new = [u[h] - jnp.dot(w[h], state_ref[h], preferred_element_type=f32) for h in heads]
    for h in heads:
        bc = bcol_ref[h]
        state = state_ref[h]
        o = (jnp.dot(q_ref[:, h * dk:(h + 1) * dk] * jnp.exp(bc), state, preferred_element_type=f32)
             + jnp.dot(qk[h], v_new[h], preferred_element_type=f32))
        b_last = bc[C - 1:C, :]
        state_ref[h] = state * jnp.exp(b_last) + lax.dot_general(head_k(h) * jnp.exp(b_last - bc), v_new[h], tn,
                                                                 preferred_element_type=f32)
        z = z_ref[:, h * dv:(h + 1) * dv]
        o = o * lax.rsqrt(jnp.mean(o * o, axis=-1, keepdims=True) + EPS)
        o_ref[:, h * dv:(h + 1) * dv] = o * gain_ref[...] * (z * jax.nn.sigmoid(z))


def gated_delta_gated(q, k, v, log_a, beta, z, gain, n_heads):
    B, S, _ = q.shape
    dk = q.shape[-1] // n_heads
    dv = v.shape[-1] // n_heads
    n = S // CHUNK
    b = jnp.cumsum(log_a.astype(jnp.float32).transpose(0, 2, 1).reshape(B, n_heads, n, CHUNK), axis=-1)
    b_col = b.reshape(B, n_heads, S, 1)
    b_row = b.reshape(B, n_heads, n, 1, CHUNK)
    beta_col = beta.astype(jnp.float32).transpose(0, 2, 1).reshape(B, n_heads, S, 1)
    wide = lambda d: pl.BlockSpec((None, CHUNK, n_heads * d), lambda bi, c: (bi, c, 0))
    col_spec = pl.BlockSpec((None, n_heads, CHUNK, 1), lambda bi, c: (bi, 0, c, 0))
    return pl.pallas_call(
        _gated_delta_body,
        grid=(B, n),
        in_specs=[wide(dk), wide(dk), wide(dv), wide(dv), col_spec,
                  pl.BlockSpec((None, n_heads, None, 1, CHUNK), lambda bi, c: (bi, 0, c, 0, 0)),
                  col_spec, pl.BlockSpec((1, dv), lambda bi, c: (0, 0))],
        out_specs=wide(dv),
        out_shape=jax.ShapeDtypeStruct(v.shape, jnp.float32),
        scratch_shapes=[pltpu.VMEM((n_heads, dk, dv), jnp.float32)],
        compiler_params=pltpu.CompilerParams(dimension_semantics=("arbitrary", "arbitrary")),
        name="gated_delta",
    )(q, k, v, z, b_col, b_row, beta_col, gain.reshape(1, dv).astype(jnp.float32))


RWKV_TIME_BLOCK = 16


def _rwkv7_scan_body(r_ref, w_ref, k_ref, v_ref, kk_ref, a_ref, o_ref, s_ref, ew_ref, kka_ref):
    ts, n, _ = r_ref.shape

    @pl.when(pl.program_id(0) == 0)
    def _():
        s_ref[...] = jnp.zeros_like(s_ref)

    ew_ref[...] = jnp.exp(w_ref[...])
    kka_ref[...] = kk_ref[...] * a_ref[...]

    def step(t, carry):
        sa = s_ref[0] * kk_ref[t, pl.ds(0, 1), :]
        for j in range(1, n):
            sa = sa + s_ref[j] * kk_ref[t, pl.ds(j, 1), :]
        v_t = v_ref[t]
        o = None
        for j in range(n):
            s_new = (s_ref[j] * ew_ref[t, pl.ds(j, 1), :] - sa * kka_ref[t, pl.ds(j, 1), :]
                     + v_t * k_ref[t, pl.ds(j, 1), :])
            s_ref[j] = s_new
            term = s_new * r_ref[t, pl.ds(j, 1), :]
            o = term if o is None else o + term
        o_ref[t] = o
        return carry

    lax.fori_loop(0, ts, step, 0)


def rwkv7_scan(r, log_w, k, v, kk, a):
    B, S, H, N = r.shape
    L = B * H

    def to_lanes(t):
        return t.transpose(1, 3, 0, 2).reshape(S, N, L)

    ts = RWKV_TIME_BLOCK
    spec = pl.BlockSpec((ts, N, L), lambda i: (i, 0, 0))
    o = pl.pallas_call(
        _rwkv7_scan_body,
        grid=(S // ts,),
        in_specs=[spec] * 6,
        out_specs=spec,
        out_shape=jax.ShapeDtypeStruct((S, N, L), jnp.float32),
        scratch_shapes=[pltpu.VMEM((N, N, L), jnp.float32),
                        pltpu.VMEM((ts, N, L), jnp.float32),
                        pltpu.VMEM((ts, N, L), jnp.float32)],
        compiler_params=pltpu.CompilerParams(dimension_semantics=("arbitrary",)),
        name="rwkv7_scan",
    )(*(to_lanes(t) for t in (r, log_w, k, v, kk, a)))
    return o.reshape(S, N, B, H).transpose(2, 0, 3, 1)


def even_mixer(h, norm_gain, w_in, gk_w2, gk_b, gla_norm, conv_w, a_log, dt_bias, gdn_norm, w_out):
    f32 = jnp.float32
    proj = dense(h, w_in, gain=norm_gain)
    gq, gk, gv, gz, glr, bq, bk, bv, bz, ba, bb = split_cols(proj, EVEN_COLS)
    log_g = jax.nn.log_sigmoid(dense(glr, gk_w2) + gk_b.astype(f32)) / GLA_GATE_NORM
    o_a = chunk_gla_gated(gq * GLA_DK ** -0.5, gk, gv, log_g, gz, gla_norm, GLA_HEADS)
    qkv = jax.nn.silu(causal_dwconv(jnp.concatenate([bq, bk, bv], axis=-1), conv_w.astype(f32)))
    cq, ck, cv = split_cols(qkv, (GDN_QK, GDN_QK, GDN_VW))
    log_a = -jnp.exp(a_log.astype(f32)) * jax.nn.softplus(ba + dt_bias.astype(f32))
    beta = jax.nn.sigmoid(bb)

    def head_l2norm(t):
        return l2norm(t.reshape(t.shape[:2] + (GDN_HEADS, GDN_DK))).reshape(t.shape)

    o_b = gated_delta_gated(head_l2norm(cq) * GDN_DK ** -0.5, head_l2norm(ck), cv, log_a, beta, bz, gdn_norm,
                            GDN_HEADS)
    y = jnp.concatenate([o_a, o_b], axis=-1)
    return dense(y, w_out)


def odd_mixer(h, norm_gain, w_in, lb, hg_norm, mu, w0, w2, a0, a2, g2, k_k, k_a, r_k, ln_w, ln_b, w_out):
    f32 = jnp.float32
    B, S, _ = h.shape
    proj = dense(h, w_in, gain=norm_gain)
    hq, hf, hi, hg = split_cols(proj[..., :HGRN_IN], HGRN_COLS)
    lb = lb.astype(f32)
    fgate = lb + (1.0 - lb) * jax.nn.sigmoid(hf)
    o_h = chunk_gla_gated(jax.nn.silu(hq), 1.0 - fgate, hi, jnp.log(fgate), hg, hg_norm, HGRN_HEADS)

    def hs(t):
        return t.reshape(B, S, RWKV_HEADS, RWKV_HD)
    r, k, v, w_lr, a_lr, g_lr = split_cols(token_shift(proj[..., HGRN_IN:], mu.astype(f32)), RWKV_COLS)
    log_w = -jnp.exp(-jax.nn.softplus(-(w0.astype(f32) + dense(jnp.tanh(w_lr), w2))) - 0.5)
    a = jax.nn.sigmoid(a0.astype(f32) + dense(a_lr, a2))
    g = dense(jax.nn.sigmoid(g_lr), g2)
    kk = l2norm(hs(k * k_k.astype(f32)))
    k = k * (1.0 + (a - 1.0) * k_a.astype(f32))
    rh, kh, vh = hs(r), hs(k), hs(v)
    o = rwkv7_scan(rh, hs(log_w), kh, vh, kk, hs(a))
    mean = jnp.mean(o, axis=-1, keepdims=True)
    var = jnp.mean(jnp.square(o - mean), axis=-1, keepdims=True)
    o = ((o - mean) * lax.rsqrt(var + RWKV_GN_EPS)).reshape(B, S, RW_W) * ln_w.astype(f32) + ln_b.astype(f32)
    bonus = jnp.sum(rh * kh * r_k.astype(f32), axis=-1, keepdims=True) * vh
    o_r = (o + bonus.reshape(B, S, RW_W)) * g
    y = jnp.concatenate([o_h, o_r], axis=-1)
    return dense(y, w_out)


LANES = 128
SUBLANES = 8
TOKENS_PER_VREG = SUBLANES * LANES


def _take_top(w_ref, n, count, emit, extra_ref=None):
    def one_round(i, prev):
        best = jnp.full((SUBLANES, LANES), -jnp.inf, jnp.float32)
        where = jnp.zeros((SUBLANES, LANES), jnp.int32)
        extra = jnp.zeros((SUBLANES, LANES), jnp.int32)
        for c in range(n):
            val = jnp.where(prev == c, -jnp.inf, w_ref[c])
            w_ref[c] = val
            better = val > best
            best = jnp.where(better, val, best)
            where = jnp.where(better, c, where)
            if extra_ref is not None:
                extra = jnp.where(better, extra_ref[c], extra)
        emit(i, best, where, extra)
        return where

    lax.fori_loop(0, count, one_round, jnp.full((SUBLANES, LANES), -1, jnp.int32))


def _peer_topk_body(s0_ref, s1_ref, idx_ref, gate_ref, w_ref, val_ref, pos_ref, cand_ref, cand_id_ref):
    nk = s0_ref.shape[0]
    topk = idx_ref.shape[0]
    for half, s_ref in enumerate((s0_ref, s1_ref)):
        w_ref[...] = s_ref[...]

        def emit_half(i, value, index, _, half=half):
            val_ref[half, i] = value
            pos_ref[half, i] = index

        _take_top(w_ref, nk, topk, emit_half)

    pairs = [(a, b) for a in range(topk) for b in range(topk) if (a + 1) * (b + 1) <= topk]
    for c, (a, b) in enumerate(pairs):
        cand_ref[c] = val_ref[0, a] + val_ref[1, b]
        cand_id_ref[c] = pos_ref[0, a] * nk + pos_ref[1, b]

    def emit_pair(i, value, _, expert):
        gate_ref[i] = value
        idx_ref[i] = expert

    _take_top(cand_ref, len(pairs), topk, emit_pair, extra_ref=cand_id_ref)

    best = gate_ref[...]
    e = jnp.exp(best - best[0:1])
    gate_ref[...] = e / jnp.sum(e, axis=0, keepdims=True)


def peer_topk(scores_t):
    H, _, NK, T = scores_t.shape
    K = PEER_TOPK
    G = T // TOKENS_PER_VREG
    npairs = sum(1 for a in range(K) for b in range(K) if (a + 1) * (b + 1) <= K)
    s = scores_t.reshape(H * 2, NK, G, SUBLANES, LANES)
    in_block = (None, NK, None, SUBLANES, LANES)
    out_block = (None, K, None, SUBLANES, LANES)
    idx_t, gate_t = pl.pallas_call(
        _peer_topk_body,
        grid=(H, G),
        in_specs=[pl.BlockSpec(in_block, lambda h, g: (2 * h, 0, g, 0, 0)),
                  pl.BlockSpec(in_block, lambda h, g: (2 * h + 1, 0, g, 0, 0))],
        out_specs=[pl.BlockSpec(out_block, lambda h, g: (h, 0, g, 0, 0)),
                   pl.BlockSpec(out_block, lambda h, g: (h, 0, g, 0, 0))],
        out_shape=[jax.ShapeDtypeStruct((H, K, G, SUBLANES, LANES), jnp.int32),
                   jax.ShapeDtypeStruct((H, K, G, SUBLANES, LANES), jnp.float32)],
        scratch_shapes=[pltpu.VMEM((NK, SUBLANES, LANES), jnp.float32),
                        pltpu.VMEM((2, K, SUBLANES, LANES), jnp.float32),
                        pltpu.VMEM((2, K, SUBLANES, LANES), jnp.int32),
                        pltpu.VMEM((npairs, SUBLANES, LANES), jnp.float32),
                        pltpu.VMEM((npairs, SUBLANES, LANES), jnp.int32)],
        compiler_params=pltpu.CompilerParams(dimension_semantics=("arbitrary", "arbitrary")),
        name="peer_topk",
    )(s, s)
    to_tokens = lambda t: t.transpose(2, 3, 4, 0, 1).reshape(T, H * K)
    return to_tokens(idx_t), to_tokens(gate_t)


def peer(h, norm_gain, w_q, sub_keys, u, v):
    B, S, D = h.shape
    T = B * S
    half = PEER_DKEY // 2
    q, x = dense(h.reshape(T, D), w_q, gain=norm_gain, return_normed=True)
    q = q.reshape(T, PEER_HEADS, 2, half)
    scores_t = jnp.einsum('thpd,hpnd->hpnt', q, sub_keys.astype(jnp.float32))
    idx, gate = peer_topk(scores_t)
    out = peer_experts(x, idx, gate, u, v)
    return out.reshape(B, S, D)


PEER_TOKEN_BLOCK = 512
PEER_GROUP = SUBLANES
PEER_UNROLL = 32
PEER_DOT_DMAS = 20
PEER_AXPY_CHAINS = 4


def _peer_experts_body(idx_hbm, x_ref, g_ref, uv_hbm, o_ref, idx_smem, buf, r_scr, sb_scr, sem, idx_sem):
    tb, rows, _ = x_ref.shape
    npair = g_ref.shape[1]
    grp = buf.shape[0] // 2
    ngroups = tb // grp
    half = rows // 2
    nbody = npair // PEER_UNROLL
    dot_dmas, axpy_dmas = PEER_DOT_DMAS, PEER_UNROLL - PEER_DOT_DMAS
    dot_at = {(n * PEER_UNROLL) // dot_dmas: n for n in range(dot_dmas)}
    axpy_at = {(n * PEER_UNROLL) // axpy_dmas: n for n in range(axpy_dmas)}
    i = pl.program_id(0)

    idx_copy = pltpu.make_async_copy(idx_hbm.at[pl.ds(i * (tb * npair), tb * npair)], idx_smem, idx_sem)
    idx_copy.start()
    idx_copy.wait()

    def row_copy(tok, k, slot):
        e = idx_smem[tok * npair + k]
        return pltpu.make_async_copy(uv_hbm.at[e], buf.at[slot, k], sem.at[slot])

    def ahead_copy(tok, first, count, slot, n):
        e = idx_smem[tok * npair + first + n]
        return pltpu.make_async_copy(uv_hbm.at[e], buf.at[slot, pl.ds(first, count)].at[n], sem.at[slot])

    def slot_wait(slot):
        pltpu.make_async_copy(uv_hbm.at[pl.ds(0, npair)], buf.at[slot], sem.at[slot]).wait()

    for tok in range(grp):
        def issue_first(k, c, tok=tok):
            row_copy(tok, k, tok).start()
            return c
        lax.fori_loop(0, npair, issue_first, 0, unroll=PEER_UNROLL)

    lane = lax.broadcasted_iota(jnp.int32, (npair, LANES), 1)
    high_half = jnp.uint32(0xFFFF0000)

    def group_pair(gg, carry):
        for par in range(2):
            g = gg * 2 + par
            g_next = jnp.minimum(g + 1, ngroups - 1)
            cur, nxt = par * grp, (1 - par) * grp

            for tok in range(grp):
                t = g * grp + tok
                t_next = g_next * grp + tok
                slot_wait(cur + tok)
                x_lo = x_ref[t, pl.ds(0, half), :]
                x_hi = x_ref[t, pl.ds(half, half), :]

                def dot_body(j, c, tok=tok, t_next=t_next, x_lo=x_lo, x_hi=x_hi):
                    for q in range(PEER_UNROLL):
                        k = j * PEER_UNROLL + q
                        if q in dot_at:
                            ahead_copy(t_next, j * dot_dmas, dot_dmas, nxt + tok,
                                       dot_at[q]).start(priority=dot_at[q] % 2)
                        u_lo = pltpu.bitcast(buf[cur + tok, k, pl.ds(0, half), :] & high_half, jnp.float32)
                        u_hi = pltpu.bitcast(buf[cur + tok, k, pl.ds(half, half), :] & high_half, jnp.float32)
                        r_scr[tok, pl.ds(k, 1), :] = jnp.sum(x_lo * u_lo + x_hi * u_hi, axis=0, keepdims=True)
                    return c

                lax.fori_loop(0, nbody, dot_body, 0)

            d = jnp.zeros((npair, LANES), jnp.float32)
            for tok in range(grp):
                d = jnp.where(lane == tok, jnp.sum(r_scr[tok], axis=-1, keepdims=True), d)
            g_tile = g_ref[pl.ds(pl.multiple_of(g * grp, grp), grp), :]
            g_cols = jnp.concatenate([g_tile, jnp.zeros((npair - grp, npair), jnp.float32)], axis=0).T
            s = g_cols * (0.5 * d * (1.0 + lax.erf(d * (2.0 ** -0.5))))
            for tok in range(grp):
                sb_scr[tok] = jnp.broadcast_to(s[:, tok:tok + 1], (npair, LANES))

            for tok in range(grp):
                t = g * grp + tok
                t_next = g_next * grp + tok

                def axpy_body(j, acc, tok=tok, t_next=t_next):
                    acc = list(acc)
                    for q in range(PEER_UNROLL):
                        k = j * PEER_UNROLL + q
                        if q in axpy_at:
                            ahead_copy(t_next, nbody * dot_dmas + j * axpy_dmas, axpy_dmas, nxt + tok,
                                       axpy_at[q]).start(priority=axpy_at[q] % 2)
                        sk = sb_scr[tok, pl.ds(k, 1), :]
                        v_lo = pltpu.bitcast(buf[cur + tok, k, pl.ds(0, half), :] << 16, jnp.float32)
                        v_hi = pltpu.bitcast(buf[cur + tok, k, pl.ds(half, half), :] << 16, jnp.float32)
                        c = 2 * (q % PEER_AXPY_CHAINS)
                        acc[c] = acc[c] + sk * v_lo
                        acc[c + 1] = acc[c + 1] + sk * v_hi
                    return tuple(acc)

                zero = jnp.zeros((half, LANES), jnp.float32)
                acc = lax.fori_loop(0, nbody, axpy_body, (zero,) * (2 * PEER_AXPY_CHAINS))
                o_ref[t, pl.ds(0, half), :] = sum(acc[2::2], acc[0])
                o_ref[t, pl.ds(half, half), :] = sum(acc[3::2], acc[1])
        return carry

    lax.fori_loop(0, ngroups // 2, group_pair, 0)
    for tok in range(grp):
        slot_wait(tok)


PACK_ROWS = 512


def _pack_tables_body(u_ref, v_ref, o_ref):
    hi = pltpu.bitcast(u_ref[...].astype(jnp.bfloat16).astype(jnp.float32), jnp.uint32)
    lo = pltpu.bitcast(v_ref[...].astype(jnp.bfloat16).astype(jnp.float32), jnp.uint32)
    o_ref[...] = hi | (lo >> 16)


def pack_expert_tables(u, v):
    E, D = u.shape
    spec = pl.BlockSpec((PACK_ROWS, D), lambda i: (i, 0))
    packed = pl.pallas_call(
        _pack_tables_body,
        grid=(E // PACK_ROWS,),
        in_specs=[spec, spec],
        out_specs=spec,
        out_shape=jax.ShapeDtypeStruct((E, D), jnp.uint32),
        compiler_params=pltpu.CompilerParams(dimension_semantics=("arbitrary",)),
        name="pack_expert_tables",
    )(u.astype(jnp.float32), v.astype(jnp.float32))
    return packed.reshape(E, D // LANES, LANES)


def peer_experts(x, idx, gate, u, v):
    T, D = x.shape
    P = idx.shape[1]
    rows = D // LANES
    tb = PEER_TOKEN_BLOCK
    nslot = 2 * PEER_GROUP
    out = pl.pallas_call(
        _peer_experts_body,
        grid=(T // tb,),
        in_specs=[pl.BlockSpec(memory_space=pl.ANY),
                  pl.BlockSpec((tb, rows, LANES), lambda i: (i, 0, 0)),
                  pl.BlockSpec((tb, P), lambda i: (i, 0)),
                  pl.BlockSpec(memory_space=pl.ANY)],
        out_specs=pl.BlockSpec((tb, rows, LANES), lambda i: (i, 0, 0)),
        out_shape=jax.ShapeDtypeStruct((T, rows, LANES), jnp.float32),
        scratch_shapes=[pltpu.SMEM((tb * P,), jnp.int32),
                        pltpu.VMEM((nslot, P, rows, LANES), jnp.uint32),
                        pltpu.VMEM((PEER_GROUP, P, LANES), jnp.float32),
                        pltpu.VMEM((PEER_GROUP, P, LANES), jnp.float32),
                        pltpu.SemaphoreType.DMA((nslot,)),
                        pltpu.SemaphoreType.DMA(())],
        compiler_params=pltpu.CompilerParams(dimension_semantics=("arbitrary",)),
        name="peer_experts",
    )(idx.reshape(T * P), x.reshape(T, rows, LANES), gate, pack_expert_tables(u, v))
    return out.reshape(T, D)


def _final_rmsnorm_body(x_ref, g_ref, o_ref):
    x = x_ref[...]
    y = x * lax.rsqrt(jnp.mean(x * x, axis=-1, keepdims=True) + EPS)
    o_ref[...] = y * g_ref[...]


def final_rmsnorm(h, g):
    B, S, D = h.shape
    x = h.reshape(B * S, D)
    rows = 512
    out = pl.pallas_call(
        _final_rmsnorm_body,
        grid=(B * S // rows,),
        in_specs=[pl.BlockSpec((rows, D), lambda i: (i, 0)), pl.BlockSpec((1, D), lambda i: (0, 0))],
        out_specs=pl.BlockSpec((rows, D), lambda i: (i, 0)),
        out_shape=jax.ShapeDtypeStruct((B * S, D), jnp.float32),
        name="final_rmsnorm",
    )(x, g.reshape(1, D))
    return out.reshape(B, S, D)


def kernel(x, p, ev_w_in, ev_gla_gk_w2, ev_gla_gk_b, ev_gla_norm, ev_gdn_conv, ev_gdn_a_log,
           ev_gdn_dt_bias, ev_gdn_norm, ev_w_out, od_w_in, hgrn_lb_logits, od_hgrn_norm,
           od_rwkv_mu, od_rwkv_w0, od_rwkv_w2, od_rwkv_a0, od_rwkv_a2, od_rwkv_g2,
           od_rwkv_k_k, od_rwkv_k_a, od_rwkv_r_k, od_rwkv_ln_w, od_rwkv_ln_b, od_w_out,
           norm_mix, norm_ffn, norm_ple, peer_w_q, peer_sub_keys, peer_u, peer_v,
           ple_w_proj, ple_w_gate, norm_final):
    f32 = jnp.float32
    sm = jax.nn.softmax(hgrn_lb_logits.astype(f32), axis=0)
    lower_bounds = jnp.cumsum(sm, axis=0) - sm[0]
    h = x
    for i in range(DEPTH):
        j = i // 2
        if i % 2 == 0:
            y = even_mixer(h, norm_mix[i], ev_w_in[j], ev_gla_gk_w2[j], ev_gla_gk_b[j], ev_gla_norm[j], ev_gdn_conv[j],
                           ev_gdn_a_log[j], ev_gdn_dt_bias[j], ev_gdn_norm[j], ev_w_out[j])
        else:
            y = odd_mixer(h, norm_mix[i], od_w_in[j], lower_bounds[i], od_hgrn_norm[j], od_rwkv_mu[j], od_rwkv_w0[j],
                          od_rwkv_w2[j], od_rwkv_a0[j], od_rwkv_a2[j], od_rwkv_g2[j], od_rwkv_k_k[j],
                          od_rwkv_k_a[j], od_rwkv_r_k[j], od_rwkv_ln_w[j], od_rwkv_ln_b[j], od_w_out[j])
        h = h + y.astype(h.dtype)
        h = h + peer(h, norm_ffn[i], peer_w_q[i], peer_sub_keys[i], peer_u[i], peer_v[i]).astype(h.dtype)
        gate = jax.nn.sigmoid(dense(h, ple_w_gate[i], gain=norm_ple[i]))
        h = h + gate * dense(p[i], ple_w_proj[i])
    return final_rmsnorm(h, norm_final)
```

```python
import math
import jax
import jax.numpy as jnp
from jax import lax
import numpy as np
from jax.experimental import pallas as pl
from jax.experimental.pallas import tpu as pltpu

D_MODEL = 2048
BATCH = 8
SEQ = 2048
DEPTH = 2

PLE_DIM = 256
CHUNK = 64
EPS = 1e-6

GLA_HEADS = 4
GLA_DV = D_MODEL // (2 * GLA_HEADS)
GLA_DK = GLA_DV // 2
GLA_GATE_RANK = 16
GLA_GATE_NORM = 16.0
GDN_DK = 128
GDN_DV = 128
GDN_HEADS = D_MODEL // (2 * GDN_DV)
CONV_K = 4
HGRN_DK = 128
HGRN_DV = 128
HGRN_HEADS = D_MODEL // (2 * HGRN_DV)
RWKV_HD = 64
RWKV_HEADS = D_MODEL // (2 * RWKV_HD)
RWKV_W_LORA = 96
RWKV_A_LORA = 96
RWKV_G_LORA = 256
RWKV_GN_EPS = 64e-5
PEER_HEADS = 8
PEER_DKEY = 256
PEER_NKEYS = 128
PEER_EXPERTS = PEER_NKEYS * PEER_NKEYS
PEER_TOPK = 16
PEER_BLOCK = 128

GLA_QK = GLA_HEADS * GLA_DK
GLA_VW = GLA_HEADS * GLA_DV
GDN_QK = GDN_HEADS * GDN_DK
GDN_VW = GDN_HEADS * GDN_DV
EVEN_COLS = (GLA_QK, GLA_QK, GLA_VW, GLA_VW, GLA_GATE_RANK, GDN_QK, GDN_QK, GDN_VW, GDN_VW, GDN_HEADS, GDN_HEADS)
EVEN_IN = sum(EVEN_COLS)
HG_W = HGRN_HEADS * HGRN_DK
HG_VW = HGRN_HEADS * HGRN_DV
RW_W = RWKV_HEADS * RWKV_HD
HGRN_COLS = (HG_W, HG_W, HG_VW, HG_VW)
RWKV_COLS = (RW_W, RW_W, RW_W, RWKV_W_LORA, RWKV_A_LORA, RWKV_G_LORA)
HGRN_IN = sum(HGRN_COLS)
RWKV_IN = sum(RWKV_COLS)
ODD_IN = HGRN_IN + RWKV_IN


DENSE_ROWS = 1024
DENSE_COLS = 512


def _dense_body(x_ref, w_ref, o_ref, xb_ref):
    @pl.when(pl.program_id(1) == 0)
    def _():
        xb_ref[...] = x_ref[...].astype(jnp.bfloat16)

    o_ref[...] = jnp.dot(xb_ref[...], w_ref[...].astype(jnp.bfloat16), preferred_element_type=jnp.float32)


def _norm_dense_body(x_ref, g_ref, w_ref, o_ref, *rest):
    xb_ref = rest[-1]

    @pl.when(pl.program_id(1) == 0)
    def _():
        x = x_ref[...]
        xn = x * lax.rsqrt(jnp.mean(x * x, axis=-1, keepdims=True) + EPS) * g_ref[...]
        if len(rest) == 2:
            rest[0][...] = xn
        xb_ref[...] = xn.astype(jnp.bfloat16)

    o_ref[...] = jnp.dot(xb_ref[...], w_ref[...].astype(jnp.bfloat16), preferred_element_type=jnp.float32)


def dense(x, w, gain=None, return_normed=False):
    lead, K = x.shape[:-1], x.shape[-1]
    N = w.shape[1]
    x2 = x.reshape(-1, K).astype(jnp.float32)
    M = x2.shape[0]
    tm = min(DENSE_ROWS // 2 if return_normed else DENSE_ROWS, M)
    tn = min(DENSE_COLS, N)
    x_spec = pl.BlockSpec((tm, K), lambda i, j: (i, 0))
    w_spec = pl.BlockSpec((K, tn), lambda i, j: (0, j))
    o_spec = pl.BlockSpec((tm, tn), lambda i, j: (i, j))
    o_shape = jax.ShapeDtypeStruct((M, N), jnp.float32)
    params = dict(grid=(pl.cdiv(M, tm), pl.cdiv(N, tn)),
                  scratch_shapes=[pltpu.VMEM((tm, K), jnp.bfloat16)],
                  compiler_params=pltpu.CompilerParams(dimension_semantics=("arbitrary", "arbitrary")))
    if gain is None:
        out = pl.pallas_call(_dense_body, in_specs=[x_spec, w_spec], out_specs=o_spec, out_shape=o_shape,
                             name="dense", **params)(x2, w)
        return out.reshape(lead + (N,))
    norm_specs = [x_spec, pl.BlockSpec((1, K), lambda i, j: (0, 0)), w_spec]
    g2 = gain.reshape(1, K).astype(jnp.float32)
    if not return_normed:
        out = pl.pallas_call(_norm_dense_body, in_specs=norm_specs, out_specs=o_spec, out_shape=o_shape,
                             name="norm_dense", **params)(x2, g2, w)
        return out.reshape(lead + (N,))
    out, xn = pl.pallas_call(
        _norm_dense_body, in_specs=norm_specs, out_specs=[o_spec, x_spec],
        out_shape=[o_shape, jax.ShapeDtypeStruct((M, K), jnp.float32)], name="norm_dense_keep", **params)(x2, g2, w)
    return out.reshape(lead + (N,)), xn.reshape(lead + (K,))


def rmsnorm(x, g):
    x32 = x.astype(jnp.float32)
    y = x32 * lax.rsqrt(jnp.mean(x32 * x32, axis=-1, keepdims=True) + EPS)
    return (y * g.astype(jnp.float32)).astype(x.dtype)


def gated_rmsnorm(o, z, g):
    o = o * lax.rsqrt(jnp.mean(o * o, axis=-1, keepdims=True) + EPS)
    return o * g.astype(jnp.float32) * jax.nn.silu(z)


def l2norm(t):
    return t * lax.rsqrt(jnp.sum(t * t, axis=-1, keepdims=True) + EPS)


def split_cols(t, sizes):
    return jnp.split(t, np.cumsum(sizes)[:-1].tolist(), axis=-1)


def to_heads(t, n_heads):
    B, S, _ = t.shape
    return t.reshape(B, S, n_heads, -1).transpose(0, 2, 1, 3)


def from_heads(t):
    B, H, S, d = t.shape
    return t.transpose(0, 2, 1, 3).reshape(B, S, H * d)


def causal_dwconv(x, w):
    return lax.conv_general_dilated(x, w[:, None, :], window_strides=(1,), padding=[(w.shape[0] - 1, 0)],
                                    dimension_numbers=('NWC', 'WIO', 'NWC'), feature_group_count=x.shape[-1])


def token_shift(t, mu):
    prev = jnp.pad(t, ((0, 0), (1, 0), (0, 0)))[:, :-1]
    return t + (prev - t) * mu


GLA_SUB = 16
GLA_HEADS_PER_STEP = 4


def _chunk_gla_body(q_ref, k_ref, v_ref, g_ref, z_ref, gain_ref, o_ref, state_ref, b_ref):
    hp, dv, dk = state_ref.shape
    C = q_ref.shape[0]
    f32 = jnp.float32
    exact = lax.Precision.HIGHEST
    nt = (((1,), (1,)), ((), ()))
    tn = (((0,), (0,)), ((), ()))
    heads = range(hp)
    key_cols = lambda h: slice(h * dk, (h + 1) * dk)
    val_cols = lambda h: slice(h * dv, (h + 1) * dv)

    @pl.when(pl.program_id(2) == 0)
    def _():
        state_ref[...] = jnp.zeros_like(state_ref)

    row = lax.broadcasted_iota(jnp.int32, (C, C), 0)
    col = lax.broadcasted_iota(jnp.int32, (C, C), 1)
    lower = (row >= col).astype(f32)
    q = [q_ref[:, key_cols(h)] for h in heads]
    k = [k_ref[:, key_cols(h)] for h in heads]
    b = [jnp.dot(lower, g_ref[:, key_cols(h)], precision=exact, preferred_element_type=f32) for h in heads]
    for h in heads:
        b_ref[h] = b[h]

    t_id = lax.broadcasted_iota(jnp.int32, (C, 1), 0)
    starts = list(range(0, C, GLA_SUB))
    col_blk = lax.broadcasted_iota(jnp.int32, (GLA_SUB, C), 1)
    sub_id = lax.broadcasted_iota(jnp.int32, (GLA_SUB, 1), 0)
    q_blk = [[q[h][lo:lo + GLA_SUB] for lo in starts] for h in heads]
    b_blk = [[b[h][lo:lo + GLA_SUB] for lo in starts] for h in heads]
    blocks = [[jnp.zeros((GLA_SUB, C), f32) for _ in starts] for h in heads]
    for n, lo in enumerate(starts):
        if lo == 0:
            continue
        for h in heads:
            r = b_ref[h, pl.ds(lo - 1, 1), :]
            k_hat = k[h] * jnp.exp(jnp.where(t_id < lo, r - b[h], 0.0))
            past = lax.dot_general(q_blk[h][n] * jnp.exp(b_blk[h][n] - r), k_hat, nt, precision=exact,
                                   preferred_element_type=f32)
            blocks[h][n] = jnp.where(col_blk < lo, past, 0.0)
    for j in range(GLA_SUB):
        keep = sub_id >= j
        for n, lo in enumerate(starts):
            for h in heads:
                s = lo + j
                diff = b_blk[h][n] - b_ref[h, pl.ds(s, 1), :]
                m = jnp.where(keep, q_blk[h][n] * k_ref[pl.ds(s, 1), key_cols(h)]
                              * jnp.exp(jnp.where(keep, diff, 0.0)), 0.0)
                blocks[h][n] = jnp.where(col_blk == s, jnp.sum(m, axis=-1, keepdims=True), blocks[h][n])
    attn = [jnp.concatenate(blocks[h], axis=0) for h in heads]

    v = [v_ref[:, val_cols(h)] for h in heads]
    o = [lax.dot_general(q[h] * jnp.exp(b[h]), state_ref[h], nt, preferred_element_type=f32) for h in heads]
    o = [o[h] + jnp.dot(attn[h], v[h], preferred_element_type=f32) for h in heads]
    for h in heads:
        b_last = b_ref[h, pl.ds(C - 1, 1), :]
        k_dec = k[h] * jnp.exp(b_last - b[h])
        state_ref[h] = state_ref[h] * jnp.exp(b_last) + lax.dot_general(v[h], k_dec, tn,
                                                                        preferred_element_type=f32)
    for h in heads:
        z = z_ref[:, val_cols(h)]
        on = o[h] * lax.rsqrt(jnp.mean(o[h] * o[h], axis=-1, keepdims=True) + EPS)
        o_ref[:, val_cols(h)] = on * gain_ref[...] * (z * jax.nn.sigmoid(z))


def chunk_gla_gated(q, k, v, log_g, z, gain, n_heads):
    B, S, _ = q.shape
    dk = q.shape[-1] // n_heads
    dv = v.shape[-1] // n_heads
    hp = GLA_HEADS_PER_STEP
    qk_spec = pl.BlockSpec((None, CHUNK, hp * dk), lambda b, h, c: (b, c, h))
    v_spec = pl.BlockSpec((None, CHUNK, hp * dv), lambda b, h, c: (b, c, h))
    return pl.pallas_call(
        _chunk_gla_body,
        grid=(B, n_heads // hp, S // CHUNK),
        in_specs=[qk_spec, qk_spec, v_spec, qk_spec, v_spec, pl.BlockSpec((1, dv), lambda b, h, c: (0, 0))],
        out_specs=v_spec,
        out_shape=jax.ShapeDtypeStruct(v.shape, jnp.float32),
        scratch_shapes=[pltpu.VMEM((hp, dv, dk), jnp.float32), pltpu.VMEM((hp, CHUNK, dk), jnp.float32)],
        compiler_params=pltpu.CompilerParams(dimension_semantics=("arbitrary", "arbitrary", "arbitrary")),
        name="chunk_gla",
    )(q, k, v, log_g, z, gain.reshape(1, dv).astype(jnp.float32))


def _gated_delta_body(q_ref, k_ref, v_ref, z_ref, bcol_ref, brow_ref, beta_ref, gain_ref, o_ref, state_ref):
    f32 = jnp.float32
    n_heads, dk, dv = state_ref.shape
    C = q_ref.shape[0]
    exact = lax.Precision.HIGHEST
    nt = (((1,), (1,)), ((), ()))
    tn = (((0,), (0,)), ((), ()))

    @pl.when(pl.program_id(1) == 0)
    def _():
        state_ref[...] = jnp.zeros_like(state_ref)

    row = lax.broadcasted_iota(jnp.int32, (C, C), 0)
    col = lax.broadcasted_iota(jnp.int32, (C, C), 1)
    causal = row >= col
    strict = row > col
    eye = (row == col).astype(f32)
    heads = range(n_heads)
    head_k = lambda h: k_ref[:, h * dk:(h + 1) * dk]
    head_v = lambda h: v_ref[:, h * dv:(h + 1) * dv]
    decay = [jnp.where(causal, jnp.exp(jnp.where(causal, bcol_ref[h] - brow_ref[h], 0.0)), 0.0) for h in heads]
    kb = [head_k(h) * beta_ref[h] for h in heads]
    power = [-jnp.where(strict, lax.dot_general(kb[h], head_k(h), nt, preferred_element_type=f32) * decay[h], 0.0)
             for h in heads]
    inv = [eye + power[h] for h in heads]
    span = 1
    while 2 * span < C:
        power = [jnp.dot(power[h], power[h], precision=exact, preferred_element_type=f32) for h in heads]
        inv = [inv[h] + jnp.dot(inv[h], power[h], precision=exact, preferred_element_type=f32) for h in heads]
        span *= 2
    u = [jnp.dot(inv[h], head_v(h) * beta_ref[h], preferred_element_type=f32) for h in heads]
    w = [jnp.dot(inv[h], kb[h] * jnp.exp(bcol_ref[h]), preferred_element_type=f32) for h in heads]
    qk = [lax.dot_general(q_ref[:, h * dk:(h + 1) * dk], head_k(h), nt, preferred_element_type=f32) * decay[h]
          for h in heads]
    v_new = [u[h] - jnp.dot(w[h], state_ref[h], preferred_element_type=f32) for h in heads]
    for h in heads:
        bc = bcol_ref[h]
        state = state_ref[h]
        o = (jnp.dot(q_ref[:, h * dk:(h + 1) * dk] * jnp.exp(bc), state, preferred_element_type=f32)
             + jnp.dot(qk[h], v_new[h], preferred_element_type=f32))
        b_last = bc[C - 1:C, :]
        state_ref[h] = state * jnp.exp(b_last) + lax.dot_general(head_k(h) * jnp.exp(b_last - bc), v_new[h], tn,
                                                                 preferred_element_type=f32)
        z = z_ref[:, h * dv:(h + 1) * dv]
        o = o * lax.rsqrt(jnp.mean(o * o, axis=-1, keepdims=True) + EPS)
        o_ref[:, h * dv:(h + 1) * dv] = o * gain_ref[...] * (z * jax.nn.sigmoid(z))


def gated_delta_gated(q, k, v, log_a, beta, z, gain, n_heads):
    B, S, _ = q.shape
    dk = q.shape[-1] // n_heads
    dv = v.shape[-1] // n_heads
    n = S // CHUNK
    b = jnp.cumsum(log_a.astype(jnp.float32).transpose(0, 2, 1).reshape(B, n_heads, n, CHUNK), axis=-1)
    b_col = b.reshape(B, n_heads, S, 1)
    b_row = b.reshape(B, n_heads, n, 1, CHUNK)
    beta_col = beta.astype(jnp.float32).transpose(0, 2, 1).reshape(B, n_heads, S, 1)
    wide = lambda d: pl.BlockSpec((None, CHUNK, n_heads * d), lambda bi, c: (bi, c, 0))
    col_spec = pl.BlockSpec((None, n_heads, CHUNK, 1), lambda bi, c: (bi, 0, c, 0))
    return pl.pallas_call(
        _gated_delta_body,
        grid=(B, n),
        in_specs=[wide(dk), wide(dk), wide(dv), wide(dv), col_spec,
                  pl.BlockSpec((None, n_heads, None, 1, CHUNK), lambda bi, c: (bi, 0, c, 0, 0)),
                  col_spec, pl.BlockSpec((1, dv), lambda bi, c: (0, 0))],
        out_specs=wide(dv),
        out_shape=jax.ShapeDtypeStruct(v.shape, jnp.float32),
        scratch_shapes=[pltpu.VMEM((n_heads, dk, dv), jnp.float32)],
        compiler_params=pltpu.CompilerParams(dimension_semantics=("arbitrary", "arbitrary")),
        name="gated_delta",
    )(q, k, v, z, b_col, b_row, beta_col, gain.reshape(1, dv).astype(jnp.float32))


RWKV_TIME_BLOCK = 16


def _rwkv7_scan_body(r_ref, w_ref, k_ref, v_ref, kk_ref, a_ref, o_ref, s_ref, ew_ref, kka_ref):
    ts, n, _ = r_ref.shape

    @pl.when(pl.program_id(0) == 0)
    def _():
        s_ref[...] = jnp.zeros_like(s_ref)

    ew_ref[...] = jnp.exp(w_ref[...])
    kka_ref[...] = kk_ref[...] * a_ref[...]

    def step(t, carry):
        sa = s_ref[0] * kk_ref[t, pl.ds(0, 1), :]
        for j in range(1, n):
            sa = sa + s_ref[j] * kk_ref[t, pl.ds(j, 1), :]
        v_t = v_ref[t]
        o = None
        for j in range(n):
            s_new = (s_ref[j] * ew_ref[t, pl.ds(j, 1), :] - sa * kka_ref[t, pl.ds(j, 1), :]
                     + v_t * k_ref[t, pl.ds(j, 1), :])
            s_ref[j] = s_new
            term = s_new * r_ref[t, pl.ds(j, 1), :]
            o = term if o is None else o + term
        o_ref[t] = o
        return carry

    lax.fori_loop(0, ts, step, 0)


def rwkv7_scan(r, log_w, k, v, kk, a):
    B, S, H, N = r.shape
    L = B * H

    def to_lanes(t):
        return t.transpose(1, 3, 0, 2).reshape(S, N, L)

    ts = RWKV_TIME_BLOCK
    spec = pl.BlockSpec((ts, N, L), lambda i: (i, 0, 0))
    o = pl.pallas_call(
        _rwkv7_scan_body,
        grid=(S // ts,),
        in_specs=[spec] * 6,
        out_specs=spec,
        out_shape=jax.ShapeDtypeStruct((S, N, L), jnp.float32),
        scratch_shapes=[pltpu.VMEM((N, N, L), jnp.float32),
                        pltpu.VMEM((ts, N, L), jnp.float32),
                        pltpu.VMEM((ts, N, L), jnp.float32)],
        compiler_params=pltpu.CompilerParams(dimension_semantics=("arbitrary",)),
        name="rwkv7_scan",
    )(*(to_lanes(t) for t in (r, log_w, k, v, kk, a)))
    return o.reshape(S, N, B, H).transpose(2, 0, 3, 1)


def even_mixer(h, norm_gain, w_in, gk_w2, gk_b, gla_norm, conv_w, a_log, dt_bias, gdn_norm, w_out):
    f32 = jnp.float32
    proj = dense(h, w_in, gain=norm_gain)
    gq, gk, gv, gz, glr, bq, bk, bv, bz, ba, bb = split_cols(proj, EVEN_COLS)
    log_g = jax.nn.log_sigmoid(dense(glr, gk_w2) + gk_b.astype(f32)) / GLA_GATE_NORM
    o_a = chunk_gla_gated(gq * GLA_DK ** -0.5, gk, gv, log_g, gz, gla_norm, GLA_HEADS)
    qkv = jax.nn.silu(causal_dwconv(jnp.concatenate([bq, bk, bv], axis=-1), conv_w.astype(f32)))
    cq, ck, cv = split_cols(qkv, (GDN_QK, GDN_QK, GDN_VW))
    log_a = -jnp.exp(a_log.astype(f32)) * jax.nn.softplus(ba + dt_bias.astype(f32))
    beta = jax.nn.sigmoid(bb)

    def head_l2norm(t):
        return l2norm(t.reshape(t.shape[:2] + (GDN_HEADS, GDN_DK))).reshape(t.shape)

    o_b = gated_delta_gated(head_l2norm(cq) * GDN_DK ** -0.5, head_l2norm(ck), cv, log_a, beta, bz, gdn_norm,
                            GDN_HEADS)
    y = jnp.concatenate([o_a, o_b], axis=-1)
    return dense(y, w_out)


def odd_mixer(h, norm_gain, w_in, lb, hg_norm, mu, w0, w2, a0, a2, g2, k_k, k_a, r_k, ln_w, ln_b, w_out):
    f32 = jnp.float32
    B, S, _ = h.shape
    proj = dense(h, w_in, gain=norm_gain)
    hq, hf, hi, hg = split_cols(proj[..., :HGRN_IN], HGRN_COLS)
    lb = lb.astype(f32)
    fgate = lb + (1.0 - lb) * jax.nn.sigmoid(hf)
    o_h = chunk_gla_gated(jax.nn.silu(hq), 1.0 - fgate, hi, jnp.log(fgate), hg, hg_norm, HGRN_HEADS)

    def hs(t):
        return t.reshape(B, S, RWKV_HEADS, RWKV_HD)
    r, k, v, w_lr, a_lr, g_lr = split_cols(token_shift(proj[..., HGRN_IN:], mu.astype(f32)), RWKV_COLS)
    log_w = -jnp.exp(-jax.nn.softplus(-(w0.astype(f32) + dense(jnp.tanh(w_lr), w2))) - 0.5)
    a = jax.nn.sigmoid(a0.astype(f32) + dense(a_lr, a2))
    g = dense(jax.nn.sigmoid(g_lr), g2)
    kk = l2norm(hs(k * k_k.astype(f32)))
    k = k * (1.0 + (a - 1.0) * k_a.astype(f32))
    rh, kh, vh = hs(r), hs(k), hs(v)
    o = rwkv7_scan(rh, hs(log_w), kh, vh, kk, hs(a))
    mean = jnp.mean(o, axis=-1, keepdims=True)
    var = jnp.mean(jnp.square(o - mean), axis=-1, keepdims=True)
    o = ((o - mean) * lax.rsqrt(var + RWKV_GN_EPS)).reshape(B, S, RW_W) * ln_w.astype(f32) + ln_b.astype(f32)
    bonus = jnp.sum(rh * kh * r_k.astype(f32), axis=-1, keepdims=True) * vh
    o_r = (o + bonus.reshape(B, S, RW_W)) * g
    y = jnp.concatenate([o_h, o_r], axis=-1)
    return dense(y, w_out)


LANES = 128
SUBLANES = 8
TOKENS_PER_VREG = SUBLANES * LANES


def _take_top(w_ref, n, count, emit, extra_ref=None):
    def one_round(i, prev):
        best = jnp.full((SUBLANES, LANES), -jnp.inf, jnp.float32)
        where = jnp.zeros((SUBLANES, LANES), jnp.int32)
        extra = jnp.zeros((SUBLANES, LANES), jnp.int32)
        for c in range(n):
            val = jnp.where(prev == c, -jnp.inf, w_ref[c])
            w_ref[c] = val
            better = val > best
            best = jnp.where(better, val, best)
            where = jnp.where(better, c, where)
            if extra_ref is not None:
                extra = jnp.where(better, extra_ref[c], extra)
        emit(i, best, where, extra)
        return where

    lax.fori_loop(0, count, one_round, jnp.full((SUBLANES, LANES), -1, jnp.int32))


def _peer_topk_body(s0_ref, s1_ref, idx_ref, gate_ref, w_ref, val_ref, pos_ref, cand_ref, cand_id_ref):
    nk = s0_ref.shape[0]
    topk = idx_ref.shape[0]
    for half, s_ref in enumerate((s0_ref, s1_ref)):
        w_ref[...] = s_ref[...]

        def emit_half(i, value, index, _, half=half):
            val_ref[half, i] = value
            pos_ref[half, i] = index

        _take_top(w_ref, nk, topk, emit_half)

    pairs = [(a, b) for a in range(topk) for b in range(topk) if (a + 1) * (b + 1) <= topk]
    for c, (a, b) in enumerate(pairs):
        cand_ref[c] = val_ref[0, a] + val_ref[1, b]
        cand_id_ref[c] = pos_ref[0, a] * nk + pos_ref[1, b]

    def emit_pair(i, value, _, expert):
        gate_ref[i] = value
        idx_ref[i] = expert

    _take_top(cand_ref, len(pairs), topk, emit_pair, extra_ref=cand_id_ref)

    best = gate_ref[...]
    e = jnp.exp(best - best[0:1])
    gate_ref[...] = e / jnp.sum(e, axis=0, keepdims=True)


def peer_topk(scores_t):
    H, _, NK, T = scores_t.shape
    K = PEER_TOPK
    G = T // TOKENS_PER_VREG
    npairs = sum(1 for a in range(K) for b in range(K) if (a + 1) * (b + 1) <= K)
    s = scores_t.reshape(H * 2, NK, G, SUBLANES, LANES)
    in_block = (None, NK, None, SUBLANES, LANES)
    out_block = (None, K, None, SUBLANES, LANES)
    idx_t, gate_t = pl.pallas_call(
        _peer_topk_body,
        grid=(H, G),
        in_specs=[pl.BlockSpec(in_block, lambda h, g: (2 * h, 0, g, 0, 0)),
                  pl.BlockSpec(in_block, lambda h, g: (2 * h + 1, 0, g, 0, 0))],
        out_specs=[pl.BlockSpec(out_block, lambda h, g: (h, 0, g, 0, 0)),
                   pl.BlockSpec(out_block, lambda h, g: (h, 0, g, 0, 0))],
        out_shape=[jax.ShapeDtypeStruct((H, K, G, SUBLANES, LANES), jnp.int32),
                   jax.ShapeDtypeStruct((H, K, G, SUBLANES, LANES), jnp.float32)],
        scratch_shapes=[pltpu.VMEM((NK, SUBLANES, LANES), jnp.float32),
                        pltpu.VMEM((2, K, SUBLANES, LANES), jnp.float32),
                        pltpu.VMEM((2, K, SUBLANES, LANES), jnp.int32),
                        pltpu.VMEM((npairs, SUBLANES, LANES), jnp.float32),
                        pltpu.VMEM((npairs, SUBLANES, LANES), jnp.int32)],
        compiler_params=pltpu.CompilerParams(dimension_semantics=("arbitrary", "arbitrary")),
        name="peer_topk",
    )(s, s)
    to_tokens = lambda t: t.transpose(2, 3, 4, 0, 1).reshape(T, H * K)
    return to_tokens(idx_t), to_tokens(gate_t)


def peer(h, norm_gain, w_q, sub_keys, u, v):
    B, S, D = h.shape
    T = B * S
    half = PEER_DKEY // 2
    q, x = dense(h.reshape(T, D), w_q, gain=norm_gain, return_normed=True)
    q = q.reshape(T, PEER_HEADS, 2, half)
    scores_t = jnp.einsum('thpd,hpnd->hpnt', q, sub_keys.astype(jnp.float32))
    idx, gate = peer_topk(scores_t)
    out = peer_experts(x, idx, gate, u, v)
    return out.reshape(B, S, D)


PEER_TOKEN_BLOCK = 512
PEER_GROUP = SUBLANES
PEER_UNROLL = 32
PEER_DOT_DMAS = 20
PEER_AXPY_CHAINS = 4


def _peer_experts_body(idx_hbm, x_ref, g_ref, uv_hbm, o_ref, idx_smem, buf, r_scr, sb_scr, sem, idx_sem):
    tb, rows, _ = x_ref.shape
    npair = g_ref.shape[1]
    grp = buf.shape[0] // 2
    ngroups = tb // grp
    half = rows // 2
    nbody = npair // PEER_UNROLL
    dot_dmas, axpy_dmas = PEER_DOT_DMAS, PEER_UNROLL - PEER_DOT_DMAS
    dot_at = {(n * PEER_UNROLL) // dot_dmas: n for n in range(dot_dmas)}
    axpy_at = {(n * PEER_UNROLL) // axpy_dmas: n for n in range(axpy_dmas)}
    i = pl.program_id(0)

    idx_copy = pltpu.make_async_copy(idx_hbm.at[pl.ds(i * (tb * npair), tb * npair)], idx_smem, idx_sem)
    idx_copy.start()
    idx_copy.wait()

    def row_copy(tok, k, slot):
        e = idx_smem[tok * npair + k]
        return pltpu.make_async_copy(uv_hbm.at[e], buf.at[slot, k], sem.at[slot])

    def ahead_copy(tok, first, count, slot, n):
        e = idx_smem[tok * npair + first + n]
        return pltpu.make_async_copy(uv_hbm.at[e], buf.at[slot, pl.ds(first, count)].at[n], sem.at[slot])

    def slot_wait(slot):
        pltpu.make_async_copy(uv_hbm.at[pl.ds(0, npair)], buf.at[slot], sem.at[slot]).wait()

    for tok in range(grp):
        def issue_first(k, c, tok=tok):
            row_copy(tok, k, tok).start()
            return c
        lax.fori_loop(0, npair, issue_first, 0, unroll=PEER_UNROLL)

    lane = lax.broadcasted_iota(jnp.int32, (npair, LANES), 1)
    high_half = jnp.uint32(0xFFFF0000)

    def group_pair(gg, carry):
        for par in range(2):
            g = gg * 2 + par
            g_next = jnp.minimum(g + 1, ngroups - 1)
            cur, nxt = par * grp, (1 - par) * grp

            for tok in range(grp):
                t = g * grp + tok
                t_next = g_next * grp + tok
                slot_wait(cur + tok)
                x_lo = x_ref[t, pl.ds(0, half), :]
                x_hi = x_ref[t, pl.ds(half, half), :]

                def dot_body(j, c, tok=tok, t_next=t_next, x_lo=x_lo, x_hi=x_hi):
                    for q in range(PEER_UNROLL):
                        k = j * PEER_UNROLL + q
                        if q in dot_at:
                            ahead_copy(t_next, j * dot_dmas, dot_dmas, nxt + tok,
                                       dot_at[q]).start(priority=dot_at[q] % 2)
                        u_lo = pltpu.bitcast(buf[cur + tok, k, pl.ds(0, half), :] & high_half, jnp.float32)
                        u_hi = pltpu.bitcast(buf[cur + tok, k, pl.ds(half, half), :] & high_half, jnp.float32)
                        r_scr[tok, pl.ds(k, 1), :] = jnp.sum(x_lo * u_lo + x_hi * u_hi, axis=0, keepdims=True)
                    return c

                lax.fori_loop(0, nbody, dot_body, 0)

            d = jnp.zeros((npair, LANES), jnp.float32)
            for tok in range(grp):
                d = jnp.where(lane == tok, jnp.sum(r_scr[tok], axis=-1, keepdims=True), d)
            g_tile = g_ref[pl.ds(pl.multiple_of(g * grp, grp), grp), :]
            g_cols = jnp.concatenate([g_tile, jnp.zeros((npair - grp, npair), jnp.float32)], axis=0).T
            s = g_cols * (0.5 * d * (1.0 + lax.erf(d * (2.0 ** -0.5))))
            for tok in range(grp):
                sb_scr[tok] = jnp.broadcast_to(s[:, tok:tok + 1], (npair, LANES))

            for tok in range(grp):
                t = g * grp + tok
                t_next = g_next * grp + tok

                def axpy_body(j, acc, tok=tok, t_next=t_next):
                    acc = list(acc)
                    for q in range(PEER_UNROLL):
                        k = j * PEER_UNROLL + q
                        if q in axpy_at:
                            ahead_copy(t_next, nbody * dot_dmas + j * axpy_dmas, axpy_dmas, nxt + tok,
                                       axpy_at[q]).start(priority=axpy_at[q] % 2)
                        sk = sb_scr[tok, pl.ds(k, 1), :]
                        v_lo = pltpu.bitcast(buf[cur + tok, k, pl.ds(0, half), :] << 16, jnp.float32)
                        v_hi = pltpu.bitcast(buf[cur + tok, k, pl.ds(half, half), :] << 16, jnp.float32)
                        c = 2 * (q % PEER_AXPY_CHAINS)
                        acc[c] = acc[c] + sk * v_lo
                        acc[c + 1] = acc[c + 1] + sk * v_hi
                    return tuple(acc)

                zero = jnp.zeros((half, LANES), jnp.float32)
                acc = lax.fori_loop(0, nbody, axpy_body, (zero,) * (2 * PEER_AXPY_CHAINS))
                o_ref[t, pl.ds(0, half), :] = sum(acc[2::2], acc[0])
                o_ref[t, pl.ds(half, half), :] = sum(acc[3::2], acc[1])
        return carry

    lax.fori_loop(0, ngroups // 2, group_pair, 0)
    for tok in range(grp):
        slot_wait(tok)


PACK_ROWS = 512


def _pack_tables_body(u_ref, v_ref, o_ref):
    hi = pltpu.bitcast(u_ref[...].astype(jnp.bfloat16).astype(jnp.float32), jnp.uint32)
    lo = pltpu.bitcast(v_ref[...].astype(jnp.bfloat16).astype(jnp.float32), jnp.uint32)
    o_ref[...] = hi | (lo >> 16)


def pack_expert_tables(u, v):
    E, D = u.shape
    spec = pl.BlockSpec((PACK_ROWS, D), lambda i: (i, 0))
    packed = pl.pallas_call(
        _pack_tables_body,
        grid=(E // PACK_ROWS,),
        in_specs=[spec, spec],
        out_specs=spec,
        out_shape=jax.ShapeDtypeStruct((E, D), jnp.uint32),
        compiler_params=pltpu.CompilerParams(dimension_semantics=("arbitrary",)),
        name="pack_expert_tables",
    )(u.astype(jnp.float32), v.astype(jnp.float32))
    return packed.reshape(E, D // LANES, LANES)


def peer_experts(x, idx, gate, u, v):
    T, D = x.shape
    P = idx.shape[1]
    rows = D // LANES
    tb = PEER_TOKEN_BLOCK
    nslot = 2 * PEER_GROUP
    out = pl.pallas_call(
        _peer_experts_body,
        grid=(T // tb,),
        in_specs=[pl.BlockSpec(memory_space=pl.ANY),
                  pl.BlockSpec((tb, rows, LANES), lambda i: (i, 0, 0)),
                  pl.BlockSpec((tb, P), lambda i: (i, 0)),
                  pl.BlockSpec(memory_space=pl.ANY)],
        out_specs=pl.BlockSpec((tb, rows, LANES), lambda i: (i, 0, 0)),
        out_shape=jax.ShapeDtypeStruct((T, rows, LANES), jnp.float32),
        scratch_shapes=[pltpu.SMEM((tb * P,), jnp.int32),
                        pltpu.VMEM((nslot, P, rows, LANES), jnp.uint32),
                        pltpu.VMEM((PEER_GROUP, P, LANES), jnp.float32),
                        pltpu.VMEM((PEER_GROUP, P, LANES), jnp.float32),
                        pltpu.SemaphoreType.DMA((nslot,)),
                        pltpu.SemaphoreType.DMA(())],
        compiler_params=pltpu.CompilerParams(dimension_semantics=("arbitrary",)),
        name="peer_experts",
    )(idx.reshape(T * P), x.reshape(T, rows, LANES), gate, pack_expert_tables(u, v))
    return out.reshape(T, D)


def _final_rmsnorm_body(x_ref, g_ref, o_ref):
    x = x_ref[...]
    y = x * lax.rsqrt(jnp.mean(x * x, axis=-1, keepdims=True) + EPS)
    o_ref[...] = y * g_ref[...]


def final_rmsnorm(h, g):
    B, S, D = h.shape
    x = h.reshape(B * S, D)
    rows = 512
    out = pl.pallas_call(
        _final_rmsnorm_body,
        grid=(B * S // rows,),
        in_specs=[pl.BlockSpec((rows, D), lambda i: (i, 0)), pl.BlockSpec((1, D), lambda i: (0, 0))],
        out_specs=pl.BlockSpec((rows, D), lambda i: (i, 0)),
        out_shape=jax.ShapeDtypeStruct((B * S, D), jnp.float32),
        name="final_rmsnorm",
    )(x, g.reshape(1, D))
    return out.reshape(B, S, D)


def kernel(x, p, ev_w_in, ev_gla_gk_w2, ev_gla_gk_b, ev_gla_norm, ev_gdn_conv, ev_gdn_a_log,
           ev_gdn_dt_bias, ev_gdn_norm, ev_w_out, od_w_in, hgrn_lb_logits, od_hgrn_norm,
           od_rwkv_mu, od_rwkv_w0, od_rwkv_w2, od_rwkv_a0, od_rwkv_a2, od_rwkv_g2,
           od_rwkv_k_k, od_rwkv_k_a, od_rwkv_r_k, od_rwkv_ln_w, od_rwkv_ln_b, od_w_out,
           norm_mix, norm_ffn, norm_ple, peer_w_q, peer_sub_keys, peer_u, peer_v,
           ple_w_proj, ple_w_gate, norm_final):
    f32 = jnp.float32
    sm = jax.nn.softmax(hgrn_lb_logits.astype(f32), axis=0)
    lower_bounds = jnp.cumsum(sm, axis=0) - sm[0]
    h = x
    for i in range(DEPTH):
        j = i // 2
        if i % 2 == 0:
            y = even_mixer(h, norm_mix[i], ev_w_in[j], ev_gla_gk_w2[j], ev_gla_gk_b[j], ev_gla_norm[j], ev_gdn_conv[j],
                           ev_gdn_a_log[j], ev_gdn_dt_bias[j], ev_gdn_norm[j], ev_w_out[j])
        else:
            y = odd_mixer(h, norm_mix[i], od_w_in[j], lower_bounds[i], od_hgrn_norm[j], od_rwkv_mu[j], od_rwkv_w0[j],
                          od_rwkv_w2[j], od_rwkv_a0[j], od_rwkv_a2[j], od_rwkv_g2[j], od_rwkv_k_k[j],
                          od_rwkv_k_a[j], od_rwkv_r_k[j], od_rwkv_ln_w[j], od_rwkv_ln_b[j], od_w_out[j])
        h = h + y.astype(h.dtype)
        h = h + peer(h, norm_ffn[i], peer_w_q[i], peer_sub_keys[i], peer_u[i], peer_v[i]).astype(h.dtype)
        gate = jax.nn.sigmoid(dense(h, ple_w_gate[i], gain=norm_ple[i]))
        h = h + gate * dense(p[i], ple_w_proj[i])
    return final_rmsnorm(h, norm_final)
```
